```python
import math
import jax, jax.numpy as jnp
from jax import lax
import numpy as np

D_MODEL = 1024
BATCH = 2
SEQ = 8192
DEPTH = 4
DEC_BATCH = 32
DEC_SEQ = 1
PAST_LEN = 8192
PAGE_SIZE = 128

N_MIXERS = 2
N_ATTN_LAYERS = (DEPTH + 1) // 2
N_GMLP_LAYERS = DEPTH // 2
N_HEADS = 8
HEAD_DIM = D_MODEL // (2 * N_HEADS)
N_QK_HEADS = 2 * N_HEADS
V_HEAD_DIM = 2 * HEAD_DIM
ROPE_THETA = 10000.0
Q_BLOCK = 128
CHUNK = 128
N_GROUPS = 8
GROUP_DIM = D_MODEL // N_GROUPS
NORM_EPS = 1e-6
SUBLN_EPS = 1e-5
LN_EPS = 1e-5

kernel_name = 'diffattn_chunkgmlp_hybrid_step'


def rms_norm(x, g, eps=NORM_EPS):
    xf = x.astype(jnp.float32)
    y = xf * lax.rsqrt(jnp.mean(xf * xf, axis=-1, keepdims=True) + eps)
    return (y * g.astype(jnp.float32)).astype(x.dtype)


def layer_norm(x, g, b, eps=LN_EPS):
    xf = x.astype(jnp.float32)
    mu = jnp.mean(xf, axis=-1, keepdims=True)
    xc = xf - mu
    y = xc * lax.rsqrt(jnp.mean(xc * xc, axis=-1, keepdims=True) + eps)
    return (y * g.astype(jnp.float32) + b.astype(jnp.float32)).astype(x.dtype)


def rope_tables(positions):
    inv = ROPE_THETA ** (-jnp.arange(0, HEAD_DIM, 2, dtype=jnp.float32) / HEAD_DIM)
    ang = positions.astype(jnp.float32)[:, None] * inv[None, :]
    ang = jnp.concatenate([ang, ang], axis=-1)
    return jnp.cos(ang), jnp.sin(ang)


def apply_rope(x, cos, sin):
    xf = x.astype(jnp.float32)
    x1, x2 = jnp.split(xf, 2, axis=-1)
    rot = jnp.concatenate([-x2, x1], axis=-1)
    return (xf * cos[:, None, :] + rot * sin[:, None, :]).astype(x.dtype)


def lambda_init(layer):
    return 0.8 - 0.6 * math.exp(-0.3 * layer)


def diff_lambda(lq1, lk1, lq2, lk2, lam0):
    f = jnp.float32
    return (jnp.exp(jnp.sum(lq1.astype(f) * lk1.astype(f)))
            - jnp.exp(jnp.sum(lq2.astype(f) * lk2.astype(f))) + lam0)


def attn_project(x, w_in, cos, sin):
    bsz, t, _ = x.shape
    q, k, v, g = jnp.split(x @ w_in, 4, axis=-1)
    q = apply_rope(q.reshape(bsz, t, N_QK_HEADS, HEAD_DIM), cos, sin)
    k = apply_rope(k.reshape(bsz, t, N_QK_HEADS, HEAD_DIM), cos, sin)
    v = v.reshape(bsz, t, N_HEADS, V_HEAD_DIM)
    return q, k, v, g


def diff_combine(p, lam):
    bsz, _, nq, nk = p.shape
    p = p.reshape(bsz, N_HEADS, 2, nq, nk)
    return p[:, :, 0] - lam * p[:, :, 1]


def diff_attn_prompt(q, k, v, lam):
    bsz, s_len = q.shape[:2]
    nb = s_len // Q_BLOCK
    scale = HEAD_DIM ** -0.5
    qb = q.reshape(bsz, nb, Q_BLOCK, N_QK_HEADS, HEAD_DIM).transpose(1, 0, 2, 3, 4)
    kf = k.astype(jnp.float32)
    key_pos = jnp.arange(s_len)

    def block(args):
        q_blk, bi = args
        s = jnp.einsum('bqhd,bkhd->bhqk', q_blk.astype(jnp.float32) * scale, kf)
        q_pos = bi * Q_BLOCK + jnp.arange(Q_BLOCK)
        s = jnp.where(key_pos[None, :] <= q_pos[:, None], s, -jnp.inf)
        a = diff_combine(jax.nn.softmax(s, axis=-1), lam)
        return jnp.einsum('bhqk,bkhe->bqhe', a.astype(v.dtype), v)

    o = lax.map(block, (qb, jnp.arange(nb)))
    return o.transpose(1, 0, 2, 3, 4).reshape(bsz, s_len, N_HEADS, V_HEAD_DIM)


def diff_attn_sample(q, k_new, v_new, cache_k, cache_v, page_table, lam):
    db, t = q.shape[:2]
    past = page_table.shape[1] * PAGE_SIZE
    k_past = cache_k[page_table].reshape(db, past, N_QK_HEADS, HEAD_DIM)
    v_past = cache_v[page_table].reshape(db, past, N_HEADS, V_HEAD_DIM)
    qf = q.astype(jnp.float32) * (HEAD_DIM ** -0.5)
    s_past = jnp.einsum('bqhd,bkhd->bhqk', qf, k_past.astype(jnp.float32))
    s_new = jnp.einsum('bqhd,bkhd->bhqk', qf, k_new.astype(jnp.float32))
    causal = jnp.tril(jnp.ones((t, t), dtype=bool))
    s_new = jnp.where(causal, s_new, -jnp.inf)
    s = jnp.concatenate([s_past, s_new], axis=-1)
    a = diff_combine(jax.nn.softmax(s, axis=-1), lam).astype(v_new.dtype)
    return (jnp.einsum('bhqk,bkhe->bqhe', a[..., :past], v_past)
            + jnp.einsum('bhqk,bkhe->bqhe', a[..., past:], v_new))


def attn_finish(o, gate, subln_g, lam0, w_out):
    bsz, t = o.shape[:2]
    o = rms_norm(o, subln_g, SUBLN_EPS) * (1.0 - lam0)
    o = o.reshape(bsz, t, N_HEADS * V_HEAD_DIM)
    return (o * jax.nn.silu(gate)) @ w_out


def chunk_spatial(v, w_s, b_s):
    bsz, t, d = v.shape
    n_c = -(-t // CHUNK)
    pad = n_c * CHUNK - t
    vp = jnp.pad(v, ((0, 0), (0, pad), (0, 0))).reshape(bsz, n_c, CHUNK, N_GROUPS, GROUP_DIM)
    causal = jnp.tril(jnp.ones((CHUNK, CHUNK), dtype=bool))
    w = jnp.where(causal[None], w_s, jnp.zeros_like(w_s)).astype(v.dtype)
    s = jnp.einsum('gij,bcjgd->bcigd', w, vp) + b_s.T[None, None, :, :, None].astype(v.dtype)
    return s.reshape(bsz, n_c * CHUNK, d)[:, :t]


def gmlp_branch(x, w_in, ln_g, ln_b, w_s, b_s, w_out):
    u, v, g = jnp.split(x @ w_in, 3, axis=-1)
    u = jax.nn.gelu(u, approximate=False)
    v = layer_norm(jax.nn.gelu(v, approximate=False), ln_g, ln_b)
    s = chunk_spatial(v, w_s, b_s)
    return (u * s * jax.nn.silu(g)) @ w_out, v


def setup_inputs(seed: int = 0) -> dict:
    key = jax.random.key(seed)
    ks = jax.random.split(key, 24)
    f = jnp.float32
    n_pages = PAST_LEN // PAGE_SIZE
    n_used = DEC_BATCH * n_pages
    n_pool = n_used + max(1, n_used // 4)
    nrm = jax.random.normal
    page_table = jax.random.permutation(ks[6], n_pool)[:n_used].reshape(DEC_BATCH, n_pages).astype(jnp.int32)
    return {
        'x_prompt': nrm(ks[0], (BATCH, SEQ, D_MODEL), f),
        'x_sample': nrm(ks[1], (DEC_BATCH, DEC_SEQ, D_MODEL), f),
        'cache_k_l0': nrm(ks[2], (n_pool, PAGE_SIZE, N_QK_HEADS, HEAD_DIM), f),
        'cache_v_l0': nrm(ks[3], (n_pool, PAGE_SIZE, N_HEADS, V_HEAD_DIM), f),
        'cache_k_l2': nrm(ks[4], (n_pool, PAGE_SIZE, N_QK_HEADS, HEAD_DIM), f),
        'cache_v_l2': nrm(ks[5], (n_pool, PAGE_SIZE, N_HEADS, V_HEAD_DIM), f),
        'page_table': page_table,
        'pre_norm_g': 1.0 + 0.05 * nrm(ks[7], (DEPTH, D_MODEL), f),
        'post_norm_g': 1.0 + 0.05 * nrm(ks[8], (DEPTH, D_MODEL), f),
        'attn_w_in': nrm(ks[9], (N_ATTN_LAYERS, D_MODEL, 4 * D_MODEL), f) * D_MODEL ** -0.5,
        'attn_lambda_q1': 0.1 * nrm(ks[10], (N_ATTN_LAYERS, HEAD_DIM), f),
        'attn_lambda_k1': 0.1 * nrm(ks[11], (N_ATTN_LAYERS, HEAD_DIM), f),
        'attn_lambda_q2': 0.1 * nrm(ks[12], (N_ATTN_LAYERS, HEAD_DIM), f),
        'attn_lambda_k2': 0.1 * nrm(ks[13], (N_ATTN_LAYERS, HEAD_DIM), f),
        'attn_subln_g': 1.0 + 0.05 * nrm(ks[14], (N_ATTN_LAYERS, V_HEAD_DIM), f),
        'attn_w_out': nrm(ks[15], (N_ATTN_LAYERS, D_MODEL, D_MODEL), f) * D_MODEL ** -0.5,
        'gmlp_w_in': nrm(ks[16], (N_GMLP_LAYERS, D_MODEL, 3 * D_MODEL), f) * D_MODEL ** -0.5,
        'gmlp_ln_g': 1.0 + 0.05 * nrm(ks[17], (N_GMLP_LAYERS, D_MODEL), f),
        'gmlp_ln_b': 0.05 * nrm(ks[18], (N_GMLP_LAYERS, D_MODEL), f),
        'gmlp_w_s': nrm(ks[19], (N_GMLP_LAYERS, N_GROUPS, CHUNK, CHUNK), f) * CHUNK ** -0.5,
        'gmlp_b_s': 1.0 + 0.05 * nrm(ks[20], (N_GMLP_LAYERS, N_GROUPS, CHUNK), f),
        'gmlp_w_out': nrm(ks[21], (N_GMLP_LAYERS, D_MODEL, D_MODEL), f) * D_MODEL ** -0.5,
    }


def reference(x_prompt, x_sample, cache_k_l0, cache_v_l0, cache_k_l2, cache_v_l2, page_table,
              pre_norm_g, post_norm_g, attn_w_in, attn_lambda_q1, attn_lambda_k1, attn_lambda_q2,
              attn_lambda_k2, attn_subln_g, attn_w_out, gmlp_w_in, gmlp_ln_g, gmlp_ln_b, gmlp_w_s,
              gmlp_b_s, gmlp_w_out):
    caches_k = (cache_k_l0, cache_k_l2)
    caches_v = (cache_v_l0, cache_v_l2)
    past = page_table.shape[1] * PAGE_SIZE
    cos_p, sin_p = rope_tables(jnp.arange(x_prompt.shape[1]))
    cos_s, sin_s = rope_tables(past + jnp.arange(x_sample.shape[1]))

    xp, xs = x_prompt, x_sample
    new_kp, new_vp, new_ks, new_vs, new_gv = [], [], [], [], []
    for i in range(DEPTH):
        hp = rms_norm(xp, pre_norm_g[i])
        hs = rms_norm(xs, pre_norm_g[i])
        li = i // N_MIXERS
        if i % N_MIXERS == 0:
            lam0 = lambda_init(i)
            lam = diff_lambda(attn_lambda_q1[li], attn_lambda_k1[li],
                              attn_lambda_q2[li], attn_lambda_k2[li], lam0)
            qp, kp, vp, gp = attn_project(hp, attn_w_in[li], cos_p, sin_p)
            yp = attn_finish(diff_attn_prompt(qp, kp, vp, lam), gp, attn_subln_g[li], lam0, attn_w_out[li])
            qs, ks_, vs_, gs = attn_project(hs, attn_w_in[li], cos_s, sin_s)
            o_s = diff_attn_sample(qs, ks_, vs_, caches_k[li], caches_v[li], page_table, lam)
            ys = attn_finish(o_s, gs, attn_subln_g[li], lam0, attn_w_out[li])
            new_kp.append(kp)
            new_vp.append(vp)
            new_ks.append(ks_)
            new_vs.append(vs_)
        else:
            yp, _ = gmlp_branch(hp, gmlp_w_in[li], gmlp_ln_g[li], gmlp_ln_b[li],
                                gmlp_w_s[li], gmlp_b_s[li], gmlp_w_out[li])
            ys, gv = gmlp_branch(hs, gmlp_w_in[li], gmlp_ln_g[li], gmlp_ln_b[li],
                                 gmlp_w_s[li], gmlp_b_s[li], gmlp_w_out[li])
            new_gv.append(gv)
        xp = xp + rms_norm(yp, post_norm_g[i])
        xs = xs + rms_norm(ys, post_norm_g[i])

    k_prompt_l0, k_prompt_l2 = new_kp
    v_prompt_l0, v_prompt_l2 = new_vp
    k_sample_l0, k_sample_l2 = new_ks
    v_sample_l0, v_sample_l2 = new_vs
    gv_sample_l1, gv_sample_l3 = new_gv
    return (xp, xs, k_prompt_l0, v_prompt_l0, k_sample_l0, v_sample_l0, gv_sample_l1,
            k_prompt_l2, v_prompt_l2, k_sample_l2, v_sample_l2, gv_sample_l3)
```

```python
import functools
import math

import jax
import jax.numpy as jnp
from jax import lax
from jax.experimental import pallas as pl
from jax.experimental.pallas import tpu as pltpu

D_MODEL = 1024
N_HEADS = 8
HEAD_DIM = 64
N_QK_HEADS = 16
V_HEAD_DIM = 128
PAGE_SIZE = 128
CHUNK = 128
N_GROUPS = 8
GROUP_DIM = 128
ROPE_THETA = 10000.0
NORM_EPS = 1e-6
SUBLN_EPS = 1e-5
LN_EPS = 1e-5

LANES = 128
VMEM_LIMIT_BYTES = 56 * 1024 * 1024

F32 = jnp.float32
BF16 = jnp.bfloat16


def _lambda_init(layer):
    return 0.8 - 0.6 * math.exp(-0.3 * layer)


def _rms(x, g, eps):
    return x * lax.rsqrt(jnp.mean(x * x, axis=-1, keepdims=True) + eps) * g


def _silu(x):
    return x * (1.0 / (1.0 + jnp.exp(-x)))


def _gelu(x):
    return 0.5 * x * (1.0 + lax.erf(x * (2.0 ** -0.5)))


def _dot(a, b):
    return jnp.dot(a, b, preferred_element_type=F32)


def _dot_nt(a, b):
    return lax.dot_general(a, b, (((1,), (1,)), ((), ())), preferred_element_type=F32)


def _lam(lq1, lk1, lq2, lk2, lam0):
    return (jnp.exp(jnp.sum(lq1[...] * lk1[...], axis=-1, keepdims=True))
            - jnp.exp(jnp.sum(lq2[...] * lk2[...], axis=-1, keepdims=True)) + lam0)


def _rope_tables(positions):
    inv = ROPE_THETA ** (-jnp.arange(0, HEAD_DIM, 2, dtype=F32) / HEAD_DIM)
    ang = positions.astype(F32)[:, None] * inv[None, :]
    cos = jnp.cos(ang)
    sin = jnp.sin(ang)
    cos2 = jnp.concatenate([cos, cos, cos, cos], axis=-1)
    sin2 = jnp.concatenate([-sin, sin, -sin, sin], axis=-1)
    return cos2, sin2


def _rope(t, cos, sin):
    lane = lax.broadcasted_iota(jnp.int32, (t.shape[0], LANES), 1)
    first_half = (lane % HEAD_DIM) < (HEAD_DIM // 2)
    outs = []
    for c in range(t.shape[1] // LANES):
        tc = t[:, c * LANES:(c + 1) * LANES]
        fwd = pltpu.roll(tc, LANES - HEAD_DIM // 2, 1)
        bwd = pltpu.roll(tc, HEAD_DIM // 2, 1)
        rot = jnp.where(first_half, fwd, bwd)
        outs.append(tc * cos + rot * sin)
    return jnp.concatenate(outs, axis=-1)


def _attn_in_kernel(x_ref, g_ref, w_ref, cos_ref, sin_ref,
                    q_ref, k32_ref, k16_ref, v32_ref, v16_ref, gate_ref):
    d = D_MODEL
    hb = _rms(x_ref[...], g_ref[...], NORM_EPS).astype(BF16)
    cos = cos_ref[...]
    sin = sin_ref[...]
    q = _rope(_dot(hb, w_ref[:, 0:d]), cos, sin)
    q_ref[...] = (q * (HEAD_DIM ** -0.5)).astype(BF16)
    k = _rope(_dot(hb, w_ref[:, d:2 * d]), cos, sin)
    k32_ref[...] = k
    k16_ref[...] = k.astype(BF16)
    v = _dot(hb, w_ref[:, 2 * d:3 * d])
    v32_ref[...] = v
    v16_ref[...] = v.astype(BF16)
    gate_ref[...] = _silu(_dot(hb, w_ref[:, 3 * d:4 * d])).astype(BF16)


def _attn_in(x, g, w16, cos, sin, tm):
    m = x.shape[0]
    n_pos_blocks = cos.shape[0] // tm
    row = lambda i: (i, 0)
    const = lambda i: (0, 0)
    outs = [jax.ShapeDtypeStruct((m, D_MODEL), dt) for dt in (BF16, F32, BF16, F32, BF16, BF16)]
    return pl.pallas_call(
        _attn_in_kernel,
        grid=(m // tm,),
        in_specs=[
            pl.BlockSpec((tm, D_MODEL), row),
            pl.BlockSpec((1, D_MODEL), const),
            pl.BlockSpec((D_MODEL, 4 * D_MODEL), const),
            pl.BlockSpec((tm, LANES), lambda i: (i % n_pos_blocks, 0)),
            pl.BlockSpec((tm, LANES), lambda i: (i % n_pos_blocks, 0)),
        ],
        out_specs=[pl.BlockSpec((tm, D_MODEL), row)] * 6,
        out_shape=outs,
        compiler_params=pltpu.CompilerParams(
            dimension_semantics=("arbitrary",), vmem_limit_bytes=VMEM_LIMIT_BYTES),
        name="attn_in",
    )(x, g, w16, cos, sin)


def _flash_kernel(q_ref, k_ref, v_ref, gate_ref, lq1, lk1, lq2, lk2, sg_ref, o_ref,
                  q2_s, m_s, l_s, acc_s, *, tq, lam0):
    qi = pl.program_id(2)
    q = q_ref[...].astype(F32)
    lane = lax.broadcasted_iota(jnp.int32, (tq, LANES), 1)
    zero = jnp.zeros_like(q)
    q2_s[0:tq, :] = jnp.where(lane < HEAD_DIM, q, zero).astype(BF16)
    q2_s[tq:2 * tq, :] = jnp.where(lane < HEAD_DIM, zero, q).astype(BF16)
    m_s[...] = jnp.full(m_s.shape, -jnp.inf, F32)
    l_s[...] = jnp.zeros(l_s.shape, F32)
    acc_s[...] = jnp.zeros(acc_s.shape, F32)

    def step(ki, masked):
        start = pl.multiple_of(ki * tq, tq)
        kc = k_ref[pl.ds(start, tq), :]
        vc = v_ref[pl.ds(start, tq), :]
        s = _dot_nt(q2_s[...], kc)
        if masked:
            r = lax.broadcasted_iota(jnp.int32, (tq, tq), 0)
            c = lax.broadcasted_iota(jnp.int32, (tq, tq), 1)
            keep = jnp.concatenate([c <= r, c <= r], axis=0)
            s = jnp.where(keep, s, -jnp.inf)
        m_old = m_s[...]
        m_new = jnp.maximum(m_old, jnp.max(s, axis=-1, keepdims=True))
        alpha = jnp.exp(m_old - m_new)
        p = jnp.exp(s - m_new)
        l_s[...] = alpha * l_s[...] + jnp.sum(p, axis=-1, keepdims=True)
        acc_s[...] = alpha * acc_s[...] + _dot(p.astype(BF16), vc)
        m_s[...] = m_new

    def body(ki, carry):
        step(ki, False)
        return carry

    lax.fori_loop(0, qi, body, 0)
    step(qi, True)

    lam = _lam(lq1, lk1, lq2, lk2, lam0)
    o1 = acc_s[0:tq, :] / l_s[0:tq, :]
    o2 = acc_s[tq:2 * tq, :] / l_s[tq:2 * tq, :]
    o = _rms(o1 - lam * o2, sg_ref[...], SUBLN_EPS) * (1.0 - lam0)
    o_ref[...] = (o * gate_ref[...].astype(F32)).astype(BF16)


def _flash(q16, k16, v16, gate16, lams, subln_g, lam0, batch, seq, tq):
    m = q16.shape[0]
    nq = seq // tq
    qmap = lambda b, h, i: (b * nq + i, h)
    kvmap = lambda b, h, i: (b, h)
    small = lambda b, h, i: (0, 0)
    lam_specs = [pl.BlockSpec((1, HEAD_DIM), small)] * 4
    return pl.pallas_call(
        functools.partial(_flash_kernel, tq=tq, lam0=lam0),
        grid=(batch, N_HEADS, nq),
        in_specs=[
            pl.BlockSpec((tq, LANES), qmap),
            pl.BlockSpec((seq, LANES), kvmap),
            pl.BlockSpec((seq, LANES), kvmap),
            pl.BlockSpec((tq, LANES), qmap),
            *lam_specs,
            pl.BlockSpec((1, V_HEAD_DIM), small),
        ],
        out_specs=pl.BlockSpec((tq, LANES), qmap),
        out_shape=jax.ShapeDtypeStruct((m, D_MODEL), BF16),
        scratch_shapes=[
            pltpu.VMEM((2 * tq, LANES), BF16),
            pltpu.VMEM((2 * tq, 1), F32),
            pltpu.VMEM((2 * tq, 1), F32),
            pltpu.VMEM((2 * tq, V_HEAD_DIM), F32),
        ],
        compiler_params=pltpu.CompilerParams(
            dimension_semantics=("arbitrary", "arbitrary", "arbitrary"),
            vmem_limit_bytes=VMEM_LIMIT_BYTES),
        name="flash_diff_attn",
    )(q16, k16, v16, gate16, *lams, subln_g)


def _decode_kernel(pt_ref, q_ref, kn_ref, vn_ref, gate_ref, lq1, lk1, lq2, lk2, sg_ref, *rest,
                   pages_per_step, lam0):
    del pt_ref
    k_refs = rest[:pages_per_step]
    v_refs = rest[pages_per_step:2 * pages_per_step]
    o_ref, m_s, l_s, acc_s = rest[2 * pages_per_step:]
    step = pl.program_id(1)
    n_steps = pl.num_programs(1)
    nh = N_QK_HEADS
    row = lax.broadcasted_iota(jnp.int32, (nh, D_MODEL), 0)
    col = lax.broadcasted_iota(jnp.int32, (nh, D_MODEL), 1)
    q = q_ref[0].astype(F32)
    qbd = jnp.where(col // HEAD_DIM == row, jnp.broadcast_to(q, (nh, D_MODEL)), 0.0)
    qbd = qbd.astype(BF16)

    @pl.when(step == 0)
    def _():
        m_s[...] = jnp.sum(qbd.astype(F32) * kn_ref[0], axis=-1, keepdims=True)
        l_s[...] = jnp.ones(l_s.shape, F32)
        acc_s[...] = jnp.broadcast_to(vn_ref[0], acc_s.shape)

    s = jnp.concatenate([_dot_nt(qbd, k_refs[j][...].astype(BF16))
                         for j in range(pages_per_step)], axis=-1)
    m_old = m_s[...]
    m_new = jnp.maximum(m_old, jnp.max(s, axis=-1, keepdims=True))
    alpha = jnp.exp(m_old - m_new)
    p = jnp.exp(s - m_new)
    l_s[...] = alpha * l_s[...] + jnp.sum(p, axis=-1, keepdims=True)
    pb = p.astype(BF16)
    pv = _dot(pb[:, 0:PAGE_SIZE], v_refs[0][...].astype(BF16))
    for j in range(1, pages_per_step):
        pv += _dot(pb[:, j * PAGE_SIZE:(j + 1) * PAGE_SIZE], v_refs[j][...].astype(BF16))
    acc_s[...] = alpha * acc_s[...] + pv
    m_s[...] = m_new

    @pl.when(step == n_steps - 1)
    def _():
        lam = _lam(lq1, lk1, lq2, lk2, lam0)
        coef = jnp.where(row % 2 == 0, 1.0, -lam)
        own = col // V_HEAD_DIM == row // 2
        t = jnp.where(own, acc_s[...] / l_s[...] * coef, 0.0)
        o = jnp.sum(t, axis=0, keepdims=True)
        hrow = lax.broadcasted_iota(jnp.int32, (N_HEADS, D_MODEL), 0)
        hcol = lax.broadcasted_iota(jnp.int32, (N_HEADS, D_MODEL), 1)
        oh = jnp.where(hcol // V_HEAD_DIM == hrow, jnp.broadcast_to(o, (N_HEADS, D_MODEL)), 0.0)
        ms = jnp.sum(oh * oh, axis=-1, keepdims=True) * (1.0 / V_HEAD_DIM)
        on = jnp.sum(oh * lax.rsqrt(ms + SUBLN_EPS), axis=0, keepdims=True)
        on = on * sg_ref[...] * (1.0 - lam0)
        o_ref[0] = (on * gate_ref[0].astype(F32)).astype(BF16)


def _decode_attn(q16, k_new, v_new, gate16, cache_k, cache_v, page_table, lams, subln_full, lam0,
                 pages_per_step):
    db = q16.shape[0]
    n_pages = page_table.shape[1]
    n_pool = cache_k.shape[0]
    ck = cache_k.reshape(n_pool, PAGE_SIZE, D_MODEL)
    cv = cache_v.reshape(n_pool, PAGE_SIZE, D_MODEL)
    tok = lambda b, s, pt: (b, 0, 0)
    small = lambda b, s, pt: (0, 0)

    def page(j):
        return lambda b, s, pt: (pt[b, s * pages_per_step + j], 0, 0)

    page_specs = [pl.BlockSpec((None, PAGE_SIZE, D_MODEL), page(j)) for j in range(pages_per_step)]
    tok_spec = pl.BlockSpec((1, 1, D_MODEL), tok)
    grid_spec = pltpu.PrefetchScalarGridSpec(
        num_scalar_prefetch=1,
        grid=(db, n_pages // pages_per_step),
        in_specs=[tok_spec, tok_spec, tok_spec, tok_spec,
                  *[pl.BlockSpec((1, HEAD_DIM), small)] * 4,
                  pl.BlockSpec((1, D_MODEL), small),
                  *page_specs, *page_specs],
        out_specs=tok_spec,
        scratch_shapes=[
            pltpu.VMEM((N_QK_HEADS, 1), F32),
            pltpu.VMEM((N_QK_HEADS, 1), F32),
            pltpu.VMEM((N_QK_HEADS, D_MODEL), F32),
        ],
    )
    r3 = lambda a: a.reshape(db, 1, D_MODEL)
    out = pl.pallas_call(
        functools.partial(_decode_kernel, pages_per_step=pages_per_step, lam0=lam0),
        grid_spec=grid_spec,
        out_shape=jax.ShapeDtypeStruct((db, 1, D_MODEL), BF16),
        compiler_params=pltpu.CompilerParams(
            dimension_semantics=("arbitrary", "arbitrary"),
            vmem_limit_bytes=VMEM_LIMIT_BYTES),
        name="paged_diff_attn",
    )(page_table, r3(q16), r3(k_new), r3(v_new), r3(gate16), *lams, subln_full,
      *([ck] * pages_per_step), *([cv] * pages_per_step))
    return out.reshape(db, D_MODEL)


def _out_proj_kernel(a_ref, w_ref, pg_ref, x_ref, o_ref):
    y = _dot(a_ref[...], w_ref[...])
    o_ref[...] = x_ref[...] + _rms(y, pg_ref[...], NORM_EPS)


def _out_proj(a16, w16, post_g, x, tm):
    m = x.shape[0]
    row = lambda i: (i, 0)
    const = lambda i: (0, 0)
    return pl.pallas_call(
        _out_proj_kernel,
        grid=(m // tm,),
        in_specs=[
            pl.BlockSpec((tm, D_MODEL), row),
            pl.BlockSpec((D_MODEL, D_MODEL), const),
            pl.BlockSpec((1, D_MODEL), const),
            pl.BlockSpec((tm, D_MODEL), row),
        ],
        out_specs=pl.BlockSpec((tm, D_MODEL), row),
        out_shape=jax.ShapeDtypeStruct((m, D_MODEL), F32),
        compiler_params=pltpu.CompilerParams(
            dimension_semantics=("arbitrary",), vmem_limit_bytes=VMEM_LIMIT_BYTES),
        name="out_proj",
    )(a16, w16, post_g, x)


def _gmlp_kernel(x_ref, g_ref, w_in_ref, lng_ref, lnb_ref, ws_ref, bs_ref, w_out_ref, pg_ref,
                 o_ref, *gv_ref, tm, single_row_chunks):
    d = D_MODEL
    x = x_ref[...]
    hb = _rms(x, g_ref[...], NORM_EPS).astype(BF16)
    u = _gelu(_dot(hb, w_in_ref[:, 0:d]))
    v = _gelu(_dot(hb, w_in_ref[:, d:2 * d]))
    mu = jnp.mean(v, axis=-1, keepdims=True)
    vc = v - mu
    v = vc * lax.rsqrt(jnp.mean(vc * vc, axis=-1, keepdims=True) + LN_EPS) * lng_ref[...] + lnb_ref[...]
    gate = _silu(_dot(hb, w_in_ref[:, 2 * d:3 * d]))
    if single_row_chunks:
        s = v * ws_ref[...] + bs_ref[...]
        gv_ref[0][...] = v
    else:
        r = lax.broadcasted_iota(jnp.int32, (CHUNK, CHUNK), 0)
        c = lax.broadcasted_iota(jnp.int32, (CHUNK, CHUNK), 1)
        vb = v.astype(BF16)
        rows = []
        for ci in range(tm // CHUNK):
            cols = []
            for gi in range(N_GROUPS):
                w = jnp.where(c <= r, ws_ref[gi], 0.0).astype(BF16)
                cols.append(_dot(w, vb[ci * CHUNK:(ci + 1) * CHUNK, gi * GROUP_DIM:(gi + 1) * GROUP_DIM]))
            rows.append(jnp.concatenate(cols, axis=-1) + bs_ref[...])
        s = jnp.concatenate(rows, axis=0)
    z = (u * s * gate).astype(BF16)
    o_ref[...] = x + _rms(_dot(z, w_out_ref[...]), pg_ref[...], NORM_EPS)


def _gmlp(x, pre_g, w_in16, ln_g, ln_b, ws, bs, w_out16, post_g, tm, single_row_chunks):
    m = x.shape[0]
    row = lambda i: (i, 0)
    const = lambda i: (0, 0)
    vec = pl.BlockSpec((1, D_MODEL), const)
    if single_row_chunks:
        ws_spec = vec
        bs_spec = vec
        out_shape = [jax.ShapeDtypeStruct((m, D_MODEL), F32)] * 2
        out_specs = [pl.BlockSpec((tm, D_MODEL), row)] * 2
    else:
        ws_spec = pl.BlockSpec((N_GROUPS, CHUNK, CHUNK), lambda i: (0, 0, 0))
        bs_spec = pl.BlockSpec((CHUNK, D_MODEL), const)
        out_shape = jax.ShapeDtypeStruct((m, D_MODEL), F32)
        out_specs = pl.BlockSpec((tm, D_MODEL), row)
    return pl.pallas_call(
        functools.partial(_gmlp_kernel, tm=tm, single_row_chunks=single_row_chunks),
        grid=(m // tm,),
        in_specs=[
            pl.BlockSpec((tm, D_MODEL), row),
            vec,
            pl.BlockSpec((D_MODEL, 3 * D_MODEL), const),
            vec, vec, ws_spec, bs_spec,
            pl.BlockSpec((D_MODEL, D_MODEL), const),
            vec,
        ],
        out_specs=out_specs,
        out_shape=out_shape,
        compiler_params=pltpu.CompilerParams(
            dimension_semantics=("arbitrary",), vmem_limit_bytes=VMEM_LIMIT_BYTES),
        name="gmlp_rows" if single_row_chunks else "gmlp_chunks",
    )(x, pre_g, w_in16, ln_g, ln_b, ws, bs, w_out16, post_g)


PROMPT_TM = 256
FLASH_TQ = 512
PAGES_PER_STEP = 4


def kernel(x_prompt, x_sample, cache_k_l0, cache_v_l0, cache_k_l2, cache_v_l2, page_table, pre_norm_g, post_norm_g, attn_w_in, attn_lambda_q1, attn_lambda_k1, attn_lambda_q2, attn_lambda_k2, attn_subln_g, attn_w_out, gmlp_w_in, gmlp_ln_g, gmlp_ln_b, gmlp_w_s, gmlp_b_s, gmlp_w_out):
    batch, seq, d = x_prompt.shape
    db, dec_seq, _ = x_sample.shape
    assert dec_seq == 1 and d == D_MODEL
    past = page_table.shape[1] * PAGE_SIZE
    depth = pre_norm_g.shape[0]
    caches = ((cache_k_l0, cache_v_l0), (cache_k_l2, cache_v_l2))

    cos_p, sin_p = _rope_tables(jnp.arange(seq))
    cos_s, sin_s = _rope_tables(jnp.full((db,), past))

    xp = x_prompt.reshape(batch * seq, d)
    xs = x_sample.reshape(db, d)
    vec = lambda a: a.reshape(1, -1)
    kp_out, vp_out, ks_out, vs_out, gv_out = [], [], [], [], []
    for i in range(depth):
        li = i // 2
        pre_g = vec(pre_norm_g[i])
        post_g = vec(post_norm_g[i])
        if i % 2 == 0:
            lam0 = _lambda_init(i)
            lams = [vec(a[li]) for a in (attn_lambda_q1, attn_lambda_k1, attn_lambda_q2, attn_lambda_k2)]
            w_in16 = attn_w_in[li].astype(BF16)
            w_out16 = attn_w_out[li].astype(BF16)
            subln = vec(attn_subln_g[li])
            q16, k32, k16, v32, v16, gate16 = _attn_in(xp, pre_g, w_in16, cos_p, sin_p, PROMPT_TM)
            og = _flash(q16, k16, v16, gate16, lams, subln, lam0, batch, seq, FLASH_TQ)
            xp = _out_proj(og, w_out16, post_g, xp, PROMPT_TM)
            kp_out.append(k32.reshape(batch, seq, N_QK_HEADS, HEAD_DIM))
            vp_out.append(v32.reshape(batch, seq, N_HEADS, V_HEAD_DIM))

            qs16, ks32, _, vs32, _, gs16 = _attn_in(xs, pre_g, w_in16, cos_s, sin_s, db)
            ck, cv = caches[li]
            ogs = _decode_attn(qs16, ks32, vs32, gs16, ck, cv, page_table, lams,
                               jnp.tile(subln, (1, N_HEADS)), lam0, PAGES_PER_STEP)
            xs = _out_proj(ogs, w_out16, post_g, xs, db)
            ks_out.append(ks32.reshape(db, 1, N_QK_HEADS, HEAD_DIM))
            vs_out.append(vs32.reshape(db, 1, N_HEADS, V_HEAD_DIM))
        else:
            w_in16 = gmlp_w_in[li].astype(BF16)
            w_out16 = gmlp_w_out[li].astype(BF16)
            ln_g = vec(gmlp_ln_g[li])
            ln_b = vec(gmlp_ln_b[li])
            ws = gmlp_w_s[li]
            bs = gmlp_b_s[li]
            bs_full = jnp.repeat(bs.T, GROUP_DIM, axis=1)
            xp = _gmlp(xp, pre_g, w_in16, ln_g, ln_b, ws, bs_full, w_out16, post_g, PROMPT_TM, False)
            ws0 = vec(jnp.repeat(ws[:, 0, 0], GROUP_DIM))
            bs0 = vec(jnp.repeat(bs[:, 0], GROUP_DIM))
            xs, gv = _gmlp(xs, pre_g, w_in16, ln_g, ln_b, ws0, bs0, w_out16, post_g, db, True)
            gv_out.append(gv.reshape(db, 1, d))

    return (xp.reshape(batch, seq, d), xs.reshape(db, 1, d),
            kp_out[0], vp_out[0], ks_out[0], vs_out[0], gv_out[0],
            kp_out[1], vp_out[1], ks_out[1], vs_out[1], gv_out[1])
```

```python
import functools
import math

import jax
import jax.numpy as jnp
from jax import lax
from jax.experimental import pallas as pl
from jax.experimental.pallas import tpu as pltpu

D_MODEL = 1024
N_HEADS = 8
HEAD_DIM = 64
N_QK_HEADS = 16
V_HEAD_DIM = 128
PAGE_SIZE = 128
CHUNK = 128
N_GROUPS = 8
GROUP_DIM = 128
ROPE_THETA = 10000.0
NORM_EPS = 1e-6
SUBLN_EPS = 1e-5
LN_EPS = 1e-5

LANES = 128
VMEM_LIMIT_BYTES = 56 * 1024 * 1024

F32 = jnp.float32
BF16 = jnp.bfloat16


def _lambda_init(layer):
    return 0.8 - 0.6 * math.exp(-0.3 * layer)


def _rms(x, g, eps):
    return x * lax.rsqrt(jnp.mean(x * x, axis=-1, keepdims=True) + eps) * g


def _silu(x):
    return x * (1.0 / (1.0 + jnp.exp(-x)))


def _gelu(x):
    return 0.5 * x * (1.0 + lax.erf(x * (2.0 ** -0.5)))


def _dot(a, b):
    return jnp.dot(a, b, preferred_element_type=F32)


def _dot_nt(a, b):
    return lax.dot_general(a, b, (((1,), (1,)), ((), ())), preferred_element_type=F32)


def _lam(lq1, lk1, lq2, lk2, lam0):
    return (jnp.exp(jnp.sum(lq1[...] * lk1[...], axis=-1, keepdims=True))
            - jnp.exp(jnp.sum(lq2[...] * lk2[...], axis=-1, keepdims=True)) + lam0)


def _rope_tables(positions):
    inv = ROPE_THETA ** (-jnp.arange(0, HEAD_DIM, 2, dtype=F32) / HEAD_DIM)
    ang = positions.astype(F32)[:, None] * inv[None, :]
    cos = jnp.cos(ang)
    sin = jnp.sin(ang)
    cos2 = jnp.concatenate([cos, cos, cos, cos], axis=-1)
    sin2 = jnp.concatenate([-sin, sin, -sin, sin], axis=-1)
    return cos2, sin2


def _rope(t, cos, sin):
    lane = lax.broadcasted_iota(jnp.int32, (t.shape[0], LANES), 1)
    first_half = (lane % HEAD_DIM) < (HEAD_DIM // 2)
    outs = []
    for c in range(t.shape[1] // LANES):
        tc = t[:, c * LANES:(c + 1) * LANES]
        fwd = pltpu.roll(tc, LANES - HEAD_DIM // 2, 1)
        bwd = pltpu.roll(tc, HEAD_DIM // 2, 1)
        rot = jnp.where(first_half, fwd, bwd)
        outs.append(tc * cos + rot * sin)
    return jnp.concatenate(outs, axis=-1)


def _attn_in_kernel(x_ref, g_ref, w_ref, cos_ref, sin_ref, *out_refs, prompt):
    d = D_MODEL
    hb = _rms(x_ref[...], g_ref[...], NORM_EPS).astype(BF16)
    cos = cos_ref[...]
    sin = sin_ref[...]
    q = _rope(_dot(hb, w_ref[:, 0:d]), cos, sin) * (HEAD_DIM ** -0.5)
    k = _rope(_dot(hb, w_ref[:, d:2 * d]), cos, sin)
    v = _dot(hb, w_ref[:, 2 * d:3 * d])
    gate = _silu(_dot(hb, w_ref[:, 3 * d:4 * d]))
    if prompt:
        q_ref, k32_ref, k16_ref, v32_ref, vt16_ref, gate_ref = out_refs
        q_ref[...] = q.astype(BF16)
        k32_ref[...] = k
        k16_ref[...] = k.astype(BF16)
        v32_ref[...] = v
        vt16_ref[0] = v.T.astype(BF16)
        gate_ref[...] = gate.astype(BF16)
    else:
        q_ref, k32_ref, v32_ref, gate_ref = out_refs
        q_ref[...] = q
        k32_ref[...] = k
        v32_ref[...] = v
        gate_ref[...] = gate


def _attn_in(x, g, w16, cos, sin, tm, kv_chunk=None):
    m = x.shape[0]
    prompt = kv_chunk is not None
    n_pos_blocks = cos.shape[0] // tm
    row = lambda i: (i, 0)
    const = lambda i: (0, 0)
    row_spec = pl.BlockSpec((tm, D_MODEL), row)
    rows = lambda dt: jax.ShapeDtypeStruct((m, D_MODEL), dt)
    if prompt:
        r = kv_chunk // tm
        out_shape = [rows(BF16), rows(F32), rows(BF16), rows(F32),
                     jax.ShapeDtypeStruct((m // kv_chunk, D_MODEL, kv_chunk), BF16), rows(BF16)]
        out_specs = [row_spec, row_spec, row_spec, row_spec,
                     pl.BlockSpec((1, D_MODEL, tm), lambda i: (i // r, 0, i % r)), row_spec]
    else:
        out_shape = [rows(F32)] * 4
        out_specs = [row_spec] * 4
    return pl.pallas_call(
        functools.partial(_attn_in_kernel, prompt=prompt),
        grid=(m // tm,),
        in_specs=[
            row_spec,
            pl.BlockSpec((1, D_MODEL), const),
            pl.BlockSpec((D_MODEL, 4 * D_MODEL), const),
            pl.BlockSpec((tm, LANES), lambda i: (i % n_pos_blocks, 0)),
            pl.BlockSpec((tm, LANES), lambda i: (i % n_pos_blocks, 0)),
        ],
        out_specs=out_specs,
        out_shape=out_shape,
        compiler_params=pltpu.CompilerParams(
            dimension_semantics=("arbitrary",), vmem_limit_bytes=VMEM_LIMIT_BYTES),
        name="attn_in_prompt" if prompt else "attn_in_decode",
    )(x, g, w16, cos, sin)


def _flash_kernel(q_ref, k_ref, vt_ref, gate_ref, lq1, lk1, lq2, lk2, sg_ref, o_ref,
                  q2_s, m_s, l_s, acc_s, *, tq, lam0):
    qi = pl.program_id(2)
    q = q_ref[...].astype(F32)
    lane = lax.broadcasted_iota(jnp.int32, (tq, LANES), 1)
    zero = jnp.zeros_like(q)
    q2_s[0:tq, :] = jnp.where(lane < HEAD_DIM, q, zero).astype(BF16)
    q2_s[tq:2 * tq, :] = jnp.where(lane < HEAD_DIM, zero, q).astype(BF16)
    m_s[...] = jnp.full(m_s.shape, -jnp.inf, F32)
    l_s[...] = jnp.zeros(l_s.shape, F32)
    acc_s[...] = jnp.zeros(acc_s.shape, F32)

    def step(ki, masked):
        start = pl.multiple_of(ki * tq, tq)
        st = _dot_nt(k_ref[pl.ds(start, tq), :], q2_s[...])
        if masked:
            key = lax.broadcasted_iota(jnp.int32, (tq, 2 * tq), 0)
            col = lax.broadcasted_iota(jnp.int32, (tq, 2 * tq), 1)
            qpos = jnp.where(col >= tq, col - tq, col)
            st = jnp.where(key <= qpos, st, -jnp.inf)
        m_old = m_s[...]
        m_new = jnp.maximum(m_old, jnp.max(st, axis=0, keepdims=True))
        alpha = jnp.exp(m_old - m_new)
        pt = jnp.exp(st - m_new)
        l_s[...] = alpha * l_s[...] + jnp.sum(pt, axis=0, keepdims=True)
        acc_s[...] = alpha * acc_s[...] + _dot(vt_ref[ki], pt.astype(BF16))
        m_s[...] = m_new

    def body(ki, carry):
        step(ki, False)
        return carry

    lax.fori_loop(0, qi, body, 0)
    step(qi, True)

    lam = _lam(lq1, lk1, lq2, lk2, lam0)
    accn = acc_s[...] / l_s[...]
    ot = accn[:, 0:tq] - lam * accn[:, tq:2 * tq]
    o = _rms(ot.T, sg_ref[...], SUBLN_EPS) * (1.0 - lam0)
    o_ref[...] = (o * gate_ref[...].astype(F32)).astype(BF16)


def _flash(q16, k16, vt16, gate16, lams, subln_g, lam0, batch, seq, tq):
    m = q16.shape[0]
    nq = seq // tq
    assert vt16.shape == (m // tq, D_MODEL, tq)
    qmap = lambda b, h, i: (b * nq + i, h)
    small = lambda b, h, i: (0, 0)
    lam_specs = [pl.BlockSpec((1, HEAD_DIM), small)] * 4
    return pl.pallas_call(
        functools.partial(_flash_kernel, tq=tq, lam0=lam0),
        grid=(batch, N_HEADS, nq),
        in_specs=[
            pl.BlockSpec((tq, LANES), qmap),
            pl.BlockSpec((seq, LANES), lambda b, h, i: (b, h)),
            pl.BlockSpec((nq, V_HEAD_DIM, tq), lambda b, h, i: (b, h, 0)),
            pl.BlockSpec((tq, LANES), qmap),
            *lam_specs,
            pl.BlockSpec((1, V_HEAD_DIM), small),
        ],
        out_specs=pl.BlockSpec((tq, LANES), qmap),
        out_shape=jax.ShapeDtypeStruct((m, D_MODEL), BF16),
        scratch_shapes=[
            pltpu.VMEM((2 * tq, LANES), BF16),
            pltpu.VMEM((1, 2 * tq), F32),
            pltpu.VMEM((1, 2 * tq), F32),
            pltpu.VMEM((V_HEAD_DIM, 2 * tq), F32),
        ],
        compiler_params=pltpu.CompilerParams(
            dimension_semantics=("arbitrary", "arbitrary", "arbitrary"),
            vmem_limit_bytes=VMEM_LIMIT_BYTES),
        name="flash_diff_attn",
    )(q16, k16, vt16, gate16, *lams, subln_g)


def _decode_kernel(pt_ref, qb_ref, knb_ref, vn_ref, gate_ref, lq1, lk1, lq2, lk2, sg_ref, *rest,
                   pages_per_step, lam0):
    del pt_ref
    np_ = pages_per_step
    k_refs = rest[:np_]
    v_refs = rest[np_:2 * np_]
    o_ref, m_s, l_s, acc_s = rest[2 * np_:]
    step = pl.program_id(1)
    n_steps = pl.num_programs(1)
    qb = qb_ref[...]

    @pl.when(step == 0)
    def _():
        m_s[...] = jnp.sum(qb * knb_ref[...], axis=1)[:, 0:1]
        l_s[...] = jnp.ones(l_s.shape, F32)
        acc_s[...] = vn_ref[...]

    s = jnp.concatenate([jnp.sum(qb * k_refs[j][...], axis=1) for j in range(np_)],
                        axis=-1)
    m_old = m_s[...]
    m_new = jnp.maximum(m_old, jnp.max(s, axis=-1, keepdims=True))
    alpha = jnp.exp(m_old - m_new)
    p = jnp.exp(s - m_new)
    l_s[...] = alpha * l_s[...] + jnp.sum(p, axis=-1, keepdims=True)
    row = lax.broadcasted_iota(jnp.int32, p.shape, 0)
    pv = None
    for hv in range(N_HEADS):
        ph = jnp.where(row // 2 == hv, p, 0.0).astype(BF16)
        vh = jnp.concatenate([r[pl.ds(hv, PAGE_SIZE, stride=N_HEADS), :] for r in v_refs],
                             axis=0).astype(BF16)
        d = _dot(ph, vh)
        pv = d if pv is None else pv + d
    acc_s[...] = alpha * acc_s[...] + pv
    m_s[...] = m_new

    @pl.when(step == n_steps - 1)
    def _():
        lam = _lam(lq1, lk1, lq2, lk2, lam0)
        acc_s[...] = acc_s[...] / l_s[...]
        even = acc_s[pl.ds(0, N_HEADS, stride=2), :]
        odd = acc_s[pl.ds(1, N_HEADS, stride=2), :]
        o = _rms(even - lam * odd, sg_ref[...], SUBLN_EPS) * (1.0 - lam0)
        o_ref[...] = o * gate_ref[...]


def _decode_attn(q32, k_new, v_new, gate32, cache_k, cache_v, page_table, lams, subln_g, lam0,
                 pages_per_step):
    db = q32.shape[0]
    n_pages = page_table.shape[1]
    kt = jnp.transpose(cache_k, (0, 2, 3, 1))
    lane_bcast = lambda a: jnp.broadcast_to(
        a.reshape(db, N_QK_HEADS, HEAD_DIM, 1), (db, N_QK_HEADS, HEAD_DIM, LANES))
    vn = jnp.repeat(v_new.reshape(db, N_HEADS, V_HEAD_DIM), 2, axis=1)
    tok4 = pl.BlockSpec((None, N_QK_HEADS, HEAD_DIM, LANES), lambda b, s, pt: (b, 0, 0, 0))
    small = lambda b, s, pt: (0, 0)

    vr = cache_v.reshape(cache_v.shape[0], PAGE_SIZE * N_HEADS, V_HEAD_DIM)

    def kpage(j):
        return lambda b, s, pt: (pt[b, s * pages_per_step + j], 0, 0, 0)

    def vpage(j):
        return lambda b, s, pt: (pt[b, s * pages_per_step + j], 0, 0)

    k_specs = [pl.BlockSpec((None, N_QK_HEADS, HEAD_DIM, PAGE_SIZE), kpage(j))
               for j in range(pages_per_step)]
    v_specs = [pl.BlockSpec((None, PAGE_SIZE * N_HEADS, V_HEAD_DIM), vpage(j))
               for j in range(pages_per_step)]
    grid_spec = pltpu.PrefetchScalarGridSpec(
        num_scalar_prefetch=1,
        grid=(db, n_pages // pages_per_step),
        in_specs=[tok4, tok4,
                  pl.BlockSpec((None, N_QK_HEADS, V_HEAD_DIM), lambda b, s, pt: (b, 0, 0)),
                  pl.BlockSpec((None, N_HEADS, V_HEAD_DIM), lambda b, s, pt: (b, 0, 0)),
                  *[pl.BlockSpec((1, HEAD_DIM), small)] * 4,
                  pl.BlockSpec((1, V_HEAD_DIM), small),
                  *k_specs, *v_specs],
        out_specs=pl.BlockSpec((None, N_HEADS, V_HEAD_DIM), lambda b, s, pt: (b, 0, 0)),
        scratch_shapes=[
            pltpu.VMEM((N_QK_HEADS, 1), F32),
            pltpu.VMEM((N_QK_HEADS, 1), F32),
            pltpu.VMEM((N_QK_HEADS, V_HEAD_DIM), F32),
        ],
    )
    out = pl.pallas_call(
        functools.partial(_decode_kernel, pages_per_step=pages_per_step, lam0=lam0),
        grid_spec=grid_spec,
        out_shape=jax.ShapeDtypeStruct((db, N_HEADS, V_HEAD_DIM), F32),
        compiler_params=pltpu.CompilerParams(
            dimension_semantics=("arbitrary", "arbitrary"),
            vmem_limit_bytes=VMEM_LIMIT_BYTES),
        name="paged_diff_attn",
    )(page_table, lane_bcast(q32), lane_bcast(k_new), vn,
      gate32.reshape(db, N_HEADS, V_HEAD_DIM), *lams, subln_g,
      *([kt] * pages_per_step), *([vr] * pages_per_step))
    return out.reshape(db, D_MODEL)


def _out_proj_kernel(a_ref, w_ref, pg_ref, x_ref, o_ref):
    y = _dot(a_ref[...].astype(BF16), w_ref[...])
    o_ref[...] = x_ref[...] + _rms(y, pg_ref[...], NORM_EPS)


def _out_proj(a, w16, post_g, x, tm):
    m = x.shape[0]
    row = lambda i: (i, 0)
    const = lambda i: (0, 0)
    return pl.pallas_call(
        _out_proj_kernel,
        grid=(m // tm,),
        in_specs=[
            pl.BlockSpec((tm, D_MODEL), row),
            pl.BlockSpec((D_MODEL, D_MODEL), const),
            pl.BlockSpec((1, D_MODEL), const),
            pl.BlockSpec((tm, D_MODEL), row),
        ],
        out_specs=pl.BlockSpec((tm, D_MODEL), row),
        out_shape=jax.ShapeDtypeStruct((m, D_MODEL), F32),
        compiler_params=pltpu.CompilerParams(
            dimension_semantics=("arbitrary",), vmem_limit_bytes=VMEM_LIMIT_BYTES),
        name="out_proj",
    )(a, w16, post_g, x)


def _gmlp_kernel(x_ref, g_ref, w_in_ref, lng_ref, lnb_ref, ws_ref, bs_ref, w_out_ref, pg_ref,
                 o_ref, *gv_ref, tm, single_row_chunks):
    d = D_MODEL
    x = x_ref[...]
    hb = _rms(x, g_ref[...], NORM_EPS).astype(BF16)
    u = _gelu(_dot(hb, w_in_ref[:, 0:d]))
    v = _gelu(_dot(hb, w_in_ref[:, d:2 * d]))
    mu = jnp.mean(v, axis=-1, keepdims=True)
    vc = v - mu
    v = vc * lax.rsqrt(jnp.mean(vc * vc, axis=-1, keepdims=True) + LN_EPS) * lng_ref[...] + lnb_ref[...]
    gate = _silu(_dot(hb, w_in_ref[:, 2 * d:3 * d]))
    if single_row_chunks:
        s = v * ws_ref[...] + bs_ref[...]
        gv_ref[0][...] = v
    else:
        r = lax.broadcasted_iota(jnp.int32, (CHUNK, CHUNK), 0)
        c = lax.broadcasted_iota(jnp.int32, (CHUNK, CHUNK), 1)
        vb = v.astype(BF16)
        rows = []
        for ci in range(tm // CHUNK):
            cols = []
            for gi in range(N_GROUPS):
                w = jnp.where(c <= r, ws_ref[gi], 0.0).astype(BF16)
                cols.append(_dot(w, vb[ci * CHUNK:(ci + 1) * CHUNK, gi * GROUP_DIM:(gi + 1) * GROUP_DIM]))
            rows.append(jnp.concatenate(cols, axis=-1) + bs_ref[...])
        s = jnp.concatenate(rows, axis=0)
    z = (u * s * gate).astype(BF16)
    o_ref[...] = x + _rms(_dot(z, w_out_ref[...]), pg_ref[...], NORM_EPS)


def _gmlp(x, pre_g, w_in16, ln_g, ln_b, ws, bs, w_out16, post_g, tm, single_row_chunks):
    m = x.shape[0]
    row = lambda i: (i, 0)
    const = lambda i: (0, 0)
    vec = pl.BlockSpec((1, D_MODEL), const)
    if single_row_chunks:
        ws_spec = vec
        bs_spec = vec
        out_shape = [jax.ShapeDtypeStruct((m, D_MODEL), F32)] * 2
        out_specs = [pl.BlockSpec((tm, D_MODEL), row)] * 2
    else:
        ws_spec = pl.BlockSpec((N_GROUPS, CHUNK, CHUNK), lambda i: (0, 0, 0))
        bs_spec = pl.BlockSpec((CHUNK, D_MODEL), const)
        out_shape = jax.ShapeDtypeStruct((m, D_MODEL), F32)
        out_specs = pl.BlockSpec((tm, D_MODEL), row)
    return pl.pallas_call(
        functools.partial(_gmlp_kernel, tm=tm, single_row_chunks=single_row_chunks),
        grid=(m // tm,),
        in_specs=[
            pl.BlockSpec((tm, D_MODEL), row),
            vec,
            pl.BlockSpec((D_MODEL, 3 * D_MODEL), const),
            vec, vec, ws_spec, bs_spec,
            pl.BlockSpec((D_MODEL, D_MODEL), const),
            vec,
        ],
        out_specs=out_specs,
        out_shape=out_shape,
        compiler_params=pltpu.CompilerParams(
            dimension_semantics=("arbitrary",), vmem_limit_bytes=VMEM_LIMIT_BYTES),
        name="gmlp_rows" if single_row_chunks else "gmlp_chunks",
    )(x, pre_g, w_in16, ln_g, ln_b, ws, bs, w_out16, post_g)


PROMPT_TM = 256
FLASH_TQ = 512
PAGES_PER_STEP = 8


def kernel(x_prompt, x_sample, cache_k_l0, cache_v_l0, cache_k_l2, cache_v_l2, page_table, pre_norm_g, post_norm_g, attn_w_in, attn_lambda_q1, attn_lambda_k1, attn_lambda_q2, attn_lambda_k2, attn_subln_g, attn_w_out, gmlp_w_in, gmlp_ln_g, gmlp_ln_b, gmlp_w_s, gmlp_b_s, gmlp_w_out):
    batch, seq, d = x_prompt.shape
    db, dec_seq, _ = x_sample.shape
    assert dec_seq == 1 and d == D_MODEL
    past = page_table.shape[1] * PAGE_SIZE
    depth = pre_norm_g.shape[0]
    caches = ((cache_k_l0, cache_v_l0), (cache_k_l2, cache_v_l2))

    cos_p, sin_p = _rope_tables(jnp.arange(seq))
    cos_s, sin_s = _rope_tables(jnp.full((db,), past))

    xp = x_prompt.reshape(batch * seq, d)
    xs = x_sample.reshape(db, d)
    vec = lambda a: a.reshape(1, -1)
    kp_out, vp_out, ks_out, vs_out, gv_out = [], [], [], [], []
    for i in range(depth):
        li = i // 2
        pre_g = vec(pre_norm_g[i])
        post_g = vec(post_norm_g[i])
        if i % 2 == 0:
            lam0 = _lambda_init(i)
            lams = [vec(a[li]) for a in (attn_lambda_q1, attn_lambda_k1, attn_lambda_q2, attn_lambda_k2)]
            w_in16 = attn_w_in[li].astype(BF16)
            w_out16 = attn_w_out[li].astype(BF16)
            subln = vec(attn_subln_g[li])
            q16, k32, k16, v32, vt16, gate16 = _attn_in(xp, pre_g, w_in16, cos_p, sin_p, PROMPT_TM,
                                                        kv_chunk=FLASH_TQ)
            og = _flash(q16, k16, vt16, gate16, lams, subln, lam0, batch, seq, FLASH_TQ)
            xp = _out_proj(og, w_out16, post_g, xp, PROMPT_TM)
            kp_out.append(k32.reshape(batch, seq, N_QK_HEADS, HEAD_DIM))
            vp_out.append(v32.reshape(batch, seq, N_HEADS, V_HEAD_DIM))

            qs32, ks32, vs32, gs32 = _attn_in(xs, pre_g, w_in16, cos_s, sin_s, db)
            ck, cv = caches[li]
            ogs = _decode_attn(qs32, ks32, vs32, gs32, ck, cv, page_table, lams, subln, lam0,
                               PAGES_PER_STEP)
            xs = _out_proj(ogs, w_out16, post_g, xs, db)
            ks_out.append(ks32.reshape(db, 1, N_QK_HEADS, HEAD_DIM))
            vs_out.append(vs32.reshape(db, 1, N_HEADS, V_HEAD_DIM))
        else:
            w_in16 = gmlp_w_in[li].astype(BF16)
            w_out16 = gmlp_w_out[li].astype(BF16)
            ln_g = vec(gmlp_ln_g[li])
            ln_b = vec(gmlp_ln_b[li])
            ws = gmlp_w_s[li]
            bs = gmlp_b_s[li]
            bs_full = jnp.repeat(bs.T, GROUP_DIM, axis=1)
            xp = _gmlp(xp, pre_g, w_in16, ln_g, ln_b, ws, bs_full, w_out16, post_g, PROMPT_TM, False)
            ws0 = vec(jnp.repeat(ws[:, 0, 0], GROUP_DIM))
            bs0 = vec(jnp.repeat(bs[:, 0], GROUP_DIM))
            xs, gv = _gmlp(xs, pre_g, w_in16, ln_g, ln_b, ws0, bs0, w_out16, post_g, db, True)
            gv_out.append(gv.reshape(db, 1, d))

    return (xp.reshape(batch, seq, d), xs.reshape(db, 1, d),
            kp_out[0], vp_out[0], ks_out[0], vs_out[0], gv_out[0],
            kp_out[1], vp_out[1], ks_out[1], vs_out[1], gv_out[1])
```

```python
import functools
import math

import jax
import jax.numpy as jnp
from jax import lax
from jax.experimental import pallas as pl
from jax.experimental.pallas import tpu as pltpu

D_MODEL = 1024
N_HEADS = 8
HEAD_DIM = 64
N_QK_HEADS = 16
V_HEAD_DIM = 128
PAGE_SIZE = 128
CHUNK = 128
N_GROUPS = 8
GROUP_DIM = 128
ROPE_THETA = 10000.0
NORM_EPS = 1e-6
SUBLN_EPS = 1e-5
LN_EPS = 1e-5

LANES = 128
VMEM_LIMIT_BYTES = 56 * 1024 * 1024

F32 = jnp.float32
BF16 = jnp.bfloat16


def _lambda_init(layer):
    return 0.8 - 0.6 * math.exp(-0.3 * layer)


def _rms(x, g, eps):
    return x * lax.rsqrt(jnp.mean(x * x, axis=-1, keepdims=True) + eps) * g


def _silu(x):
    return x * (1.0 / (1.0 + jnp.exp(-x)))


def _gelu(x):
    return 0.5 * x * (1.0 + lax.erf(x * (2.0 ** -0.5)))


def _dot(a, b):
    return jnp.dot(a, b, preferred_element_type=F32)


def _dot_nt(a, b):
    return lax.dot_general(a, b, (((1,), (1,)), ((), ())), preferred_element_type=F32)


def _lam(lq1, lk1, lq2, lk2, lam0):
    return (jnp.exp(jnp.sum(lq1[...] * lk1[...], axis=-1, keepdims=True))
            - jnp.exp(jnp.sum(lq2[...] * lk2[...], axis=-1, keepdims=True)) + lam0)


def _rope_tables(positions):
    inv = ROPE_THETA ** (-jnp.arange(0, HEAD_DIM, 2, dtype=F32) / HEAD_DIM)
    ang = positions.astype(F32)[:, None] * inv[None, :]
    cos = jnp.cos(ang)
    sin = jnp.sin(ang)
    cos2 = jnp.concatenate([cos, cos, cos, cos], axis=-1)
    sin2 = jnp.concatenate([-sin, sin, -sin, sin], axis=-1)
    return cos2, sin2


def _rope(t, cos, sin):
    lane = lax.broadcasted_iota(jnp.int32, (t.shape[0], LANES), 1)
    first_half = (lane % HEAD_DIM) < (HEAD_DIM // 2)
    outs = []
    for c in range(t.shape[1] // LANES):
        tc = t[:, c * LANES:(c + 1) * LANES]
        fwd = pltpu.roll(tc, LANES - HEAD_DIM // 2, 1)
        bwd = pltpu.roll(tc, HEAD_DIM // 2, 1)
        rot = jnp.where(first_half, fwd, bwd)
        outs.append(tc * cos + rot * sin)
    return jnp.concatenate(outs, axis=-1)


def _attn_in_kernel(x_ref, g_ref, w_ref, cos_ref, sin_ref, *out_refs, prompt):
    d = D_MODEL
    hb = _rms(x_ref[...], g_ref[...], NORM_EPS).astype(BF16)
    cos = cos_ref[...]
    sin = sin_ref[...]
    q = _rope(_dot(hb, w_ref[:, 0:d]), cos, sin) * (HEAD_DIM ** -0.5)
    k = _rope(_dot(hb, w_ref[:, d:2 * d]), cos, sin)
    v = _dot(hb, w_ref[:, 2 * d:3 * d])
    gate = _silu(_dot(hb, w_ref[:, 3 * d:4 * d]))
    if prompt:
        q_ref, kt32_ref, k16_ref, v32_ref, vt16_ref, gate_ref = out_refs
        q_ref[...] = q.astype(BF16)
        kt32_ref[...] = k.T
        k16_ref[...] = k.astype(BF16)
        v32_ref[...] = v
        vt16_ref[0] = v.T.astype(BF16)
        gate_ref[...] = gate.astype(BF16)
    else:
        q_ref, k32_ref, v32_ref, gate_ref = out_refs
        q_ref[...] = q
        k32_ref[...] = k
        v32_ref[...] = v
        gate_ref[...] = gate


def _attn_in(x, g, w16, cos, sin, tm, kv_chunk=None):
    m = x.shape[0]
    prompt = kv_chunk is not None
    n_pos_blocks = cos.shape[0] // tm
    row = lambda i: (i, 0)
    const = lambda i: (0, 0)
    row_spec = pl.BlockSpec((tm, D_MODEL), row)
    rows = lambda dt: jax.ShapeDtypeStruct((m, D_MODEL), dt)
    if prompt:
        r = kv_chunk // tm
        seq = cos.shape[0]
        out_shape = [rows(BF16), jax.ShapeDtypeStruct((m // seq, D_MODEL, seq), F32), rows(BF16),
                     rows(F32),
                     jax.ShapeDtypeStruct((m // kv_chunk, D_MODEL, kv_chunk), BF16), rows(BF16)]
        out_specs = [row_spec,
                     pl.BlockSpec((None, D_MODEL, tm),
                                  lambda i: (i // n_pos_blocks, 0, i % n_pos_blocks)),
                     row_spec, row_spec,
                     pl.BlockSpec((1, D_MODEL, tm), lambda i: (i // r, 0, i % r)), row_spec]
    else:
        out_shape = [rows(F32)] * 4
        out_specs = [row_spec] * 4
    return pl.pallas_call(
        functools.partial(_attn_in_kernel, prompt=prompt),
        grid=(m // tm,),
        in_specs=[
            row_spec,
            pl.BlockSpec((1, D_MODEL), const),
            pl.BlockSpec((D_MODEL, 4 * D_MODEL), const),
            pl.BlockSpec((tm, LANES), lambda i: (i % n_pos_blocks, 0)),
            pl.BlockSpec((tm, LANES), lambda i: (i % n_pos_blocks, 0)),
        ],
        out_specs=out_specs,
        out_shape=out_shape,
        compiler_params=pltpu.CompilerParams(
            dimension_semantics=("arbitrary",), vmem_limit_bytes=VMEM_LIMIT_BYTES),
        name="attn_in_prompt" if prompt else "attn_in_decode",
    )(x, g, w16, cos, sin)


ONES_ROWS = 16


def _flash_kernel(q_ref, k_ref, vt_ref, gate_ref, lq1, lk1, lq2, lk2, sg_ref, o_ref,
                  q2_s, sa_s, sb_s, m_s, acc_s, *, tq, lam0):
    n = pl.program_id(2)
    q = q_ref[...].astype(F32)
    lane = lax.broadcasted_iota(jnp.int32, (tq, LANES), 1)
    zero = jnp.zeros_like(q)
    q2_s[0:tq, :] = jnp.where(lane < HEAD_DIM, q, zero).astype(BF16)
    q2_s[tq:2 * tq, :] = jnp.where(lane < HEAD_DIM, zero, q).astype(BF16)
    m_s[...] = jnp.full(m_s.shape, -jnp.inf, F32)
    acc_s[...] = jnp.zeros(acc_s.shape, F32)
    ones = jnp.ones((ONES_ROWS, tq), BF16)

    def scores(ki, dst):
        start = pl.multiple_of(ki * tq, tq)
        dst[...] = _dot_nt(k_ref[pl.ds(start, tq), :], q2_s[...])

    def consume(src, ki, masked):
        st = src[...]
        if masked:
            key = lax.broadcasted_iota(jnp.int32, (tq, 2 * tq), 0)
            col = lax.broadcasted_iota(jnp.int32, (tq, 2 * tq), 1)
            qpos = jnp.where(col >= tq, col - tq, col)
            st = jnp.where(key <= qpos, st, -jnp.inf)
        m_old = m_s[...]
        m_new = jnp.maximum(m_old, jnp.max(st, axis=0, keepdims=True))
        alpha = jnp.exp(m_old - m_new)
        pt = jnp.exp(st - m_new).astype(BF16)
        vt1 = jnp.concatenate([vt_ref[ki], ones], axis=0)
        acc_s[...] = alpha * acc_s[...] + _dot(vt1, pt)
        m_s[...] = m_new

    scores(0, sa_s)

    def pair(j, carry):
        scores(2 * j + 1, sb_s)
        consume(sa_s, 2 * j, False)
        scores(2 * j + 2, sa_s)
        consume(sb_s, 2 * j + 1, False)
        return carry

    lax.fori_loop(0, lax.shift_right_logical(n, 1), pair, 0)

    @pl.when((n & 1) == 0)
    def _():
        consume(sa_s, n, True)

    @pl.when((n & 1) == 1)
    def _():
        scores(n, sb_s)
        consume(sa_s, n - 1, False)
        consume(sb_s, n, True)

    lam = _lam(lq1, lk1, lq2, lk2, lam0)
    accn = acc_s[0:V_HEAD_DIM, :] / acc_s[V_HEAD_DIM:V_HEAD_DIM + 1, :]
    ot = accn[:, 0:tq] - lam * accn[:, tq:2 * tq]
    o = _rms(ot.T, sg_ref[...], SUBLN_EPS) * (1.0 - lam0)
    o_ref[...] = (o * gate_ref[...].astype(F32)).astype(BF16)


def _flash(q16, k16, vt16, gate16, lams, subln_g, lam0, batch, seq, tq):
    m = q16.shape[0]
    nq = seq // tq
    assert vt16.shape == (m // tq, D_MODEL, tq)
    qmap = lambda b, h, i: (b * nq + i, h)
    small = lambda b, h, i: (0, 0)
    lam_specs = [pl.BlockSpec((1, HEAD_DIM), small)] * 4
    return pl.pallas_call(
        functools.partial(_flash_kernel, tq=tq, lam0=lam0),
        grid=(batch, N_HEADS, nq),
        in_specs=[
            pl.BlockSpec((tq, LANES), qmap),
            pl.BlockSpec((seq, LANES), lambda b, h, i: (b, h)),
            pl.BlockSpec((nq, V_HEAD_DIM, tq), lambda b, h, i: (b, h, 0)),
            pl.BlockSpec((tq, LANES), qmap),
            *lam_specs,
            pl.BlockSpec((1, V_HEAD_DIM), small),
        ],
        out_specs=pl.BlockSpec((tq, LANES), qmap),
        out_shape=jax.ShapeDtypeStruct((m, D_MODEL), BF16),
        scratch_shapes=[
            pltpu.VMEM((2 * tq, LANES), BF16),
            pltpu.VMEM((tq, 2 * tq), F32),
            pltpu.VMEM((tq, 2 * tq), F32),
            pltpu.VMEM((1, 2 * tq), F32),
            pltpu.VMEM((V_HEAD_DIM + ONES_ROWS, 2 * tq), F32),
        ],
        compiler_params=pltpu.CompilerParams(
            dimension_semantics=("arbitrary", "arbitrary", "arbitrary"),
            vmem_limit_bytes=VMEM_LIMIT_BYTES),
        name="flash_diff_attn",
    )(q16, k16, vt16, gate16, *lams, subln_g)


def _decode_kernel(pt_ref, qb_ref, knb_ref, vn_ref, gate_ref, lq1, lk1, lq2, lk2, sg_ref, *rest,
                   pages_per_step, lam0):
    del pt_ref
    np_ = pages_per_step
    k_refs = rest[:np_]
    v_refs = rest[np_:2 * np_]
    o_ref, m_s, l_s, acc_s = rest[2 * np_:]
    step = pl.program_id(1)
    n_steps = pl.num_programs(1)
    qb = qb_ref[...]

    @pl.when(step == 0)
    def _():
        m_s[...] = jnp.sum(qb * knb_ref[...], axis=1)[:, 0:1]
        l_s[...] = jnp.ones(l_s.shape, F32)
        acc_s[...] = vn_ref[...]

    s = jnp.concatenate([jnp.sum(qb * k_refs[j][...], axis=1) for j in range(np_)],
                        axis=-1)
    m_old = m_s[...]
    m_new = jnp.maximum(m_old, jnp.max(s, axis=-1, keepdims=True))
    alpha = jnp.exp(m_old - m_new)
    p = jnp.exp(s - m_new)
    l_s[...] = alpha * l_s[...] + jnp.sum(p, axis=-1, keepdims=True)
    row = lax.broadcasted_iota(jnp.int32, p.shape, 0)
    pv = None
    for hv in range(N_HEADS):
        ph = jnp.where(row // 2 == hv, p, 0.0).astype(BF16)
        vh = jnp.concatenate([r[pl.ds(hv, PAGE_SIZE, stride=N_HEADS), :] for r in v_refs],
                             axis=0).astype(BF16)
        d = _dot(ph, vh)
        pv = d if pv is None else pv + d
    acc_s[...] = alpha * acc_s[...] + pv
    m_s[...] = m_new

    @pl.when(step == n_steps - 1)
    def _():
        lam = _lam(lq1, lk1, lq2, lk2, lam0)
        acc_s[...] = acc_s[...] / l_s[...]
        even = acc_s[pl.ds(0, N_HEADS, stride=2), :]
        odd = acc_s[pl.ds(1, N_HEADS, stride=2), :]
        o = _rms(even - lam * odd, sg_ref[...], SUBLN_EPS) * (1.0 - lam0)
        o_ref[...] = o * gate_ref[...]


def _decode_attn(q32, k_new, v_new, gate32, cache_k, cache_v, page_table, lams, subln_g, lam0,
                 pages_per_step):
    db = q32.shape[0]
    n_pages = page_table.shape[1]
    kt = jnp.transpose(cache_k, (0, 2, 3, 1))
    lane_bcast = lambda a: jnp.broadcast_to(
        a.reshape(db, N_QK_HEADS, HEAD_DIM, 1), (db, N_QK_HEADS, HEAD_DIM, LANES))
    vn = jnp.repeat(v_new.reshape(db, N_HEADS, V_HEAD_DIM), 2, axis=1)
    tok4 = pl.BlockSpec((None, N_QK_HEADS, HEAD_DIM, LANES), lambda b, s, pt: (b, 0, 0, 0))
    small = lambda b, s, pt: (0, 0)

    vr = cache_v.reshape(cache_v.shape[0], PAGE_SIZE * N_HEADS, V_HEAD_DIM)

    def kpage(j):
        return lambda b, s, pt: (pt[b, s * pages_per_step + j], 0, 0, 0)

    def vpage(j):
        return lambda b, s, pt: (pt[b, s * pages_per_step + j], 0, 0)

    k_specs = [pl.BlockSpec((None, N_QK_HEADS, HEAD_DIM, PAGE_SIZE), kpage(j))
               for j in range(pages_per_step)]
    v_specs = [pl.BlockSpec((None, PAGE_SIZE * N_HEADS, V_HEAD_DIM), vpage(j))
               for j in range(pages_per_step)]
    grid_spec = pltpu.PrefetchScalarGridSpec(
        num_scalar_prefetch=1,
        grid=(db, n_pages // pages_per_step),
        in_specs=[tok4, tok4,
                  pl.BlockSpec((None, N_QK_HEADS, V_HEAD_DIM), lambda b, s, pt: (b, 0, 0)),
                  pl.BlockSpec((None, N_HEADS, V_HEAD_DIM), lambda b, s, pt: (b, 0, 0)),
                  *[pl.BlockSpec((1, HEAD_DIM), small)] * 4,
                  pl.BlockSpec((1, V_HEAD_DIM), small),
                  *k_specs, *v_specs],
        out_specs=pl.BlockSpec((None, N_HEADS, V_HEAD_DIM), lambda b, s, pt: (b, 0, 0)),
        scratch_shapes=[
            pltpu.VMEM((N_QK_HEADS, 1), F32),
            pltpu.VMEM((N_QK_HEADS, 1), F32),
            pltpu.VMEM((N_QK_HEADS, V_HEAD_DIM), F32),
        ],
    )
    out = pl.pallas_call(
        functools.partial(_decode_kernel, pages_per_step=pages_per_step, lam0=lam0),
        grid_spec=grid_spec,
        out_shape=jax.ShapeDtypeStruct((db, N_HEADS, V_HEAD_DIM), F32),
        compiler_params=pltpu.CompilerParams(
            dimension_semantics=("arbitrary", "arbitrary"),
            vmem_limit_bytes=VMEM_LIMIT_BYTES),
        name="paged_diff_attn",
    )(page_table, lane_bcast(q32), lane_bcast(k_new), vn,
      gate32.reshape(db, N_HEADS, V_HEAD_DIM), *lams, subln_g,
      *([kt] * pages_per_step), *([vr] * pages_per_step))
    return out.reshape(db, D_MODEL)


def _out_proj_kernel(a_ref, w_ref, pg_ref, x_ref, o_ref):
    y = _dot(a_ref[...].astype(BF16), w_ref[...])
    o_ref[...] = x_ref[...] + _rms(y, pg_ref[...], NORM_EPS)


def _out_proj(a, w16, post_g, x, tm):
    m = x.shape[0]
    row = lambda i: (i, 0)
    const = lambda i: (0, 0)
    return pl.pallas_call(
        _out_proj_kernel,
        grid=(m // tm,),
        in_specs=[
            pl.BlockSpec((tm, D_MODEL), row),
            pl.BlockSpec((D_MODEL, D_MODEL), const),
            pl.BlockSpec((1, D_MODEL), const),
            pl.BlockSpec((tm, D_MODEL), row),
        ],
        out_specs=pl.BlockSpec((tm, D_MODEL), row),
        out_shape=jax.ShapeDtypeStruct((m, D_MODEL), F32),
        compiler_params=pltpu.CompilerParams(
            dimension_semantics=("arbitrary",), vmem_limit_bytes=VMEM_LIMIT_BYTES),
        name="out_proj",
    )(a, w16, post_g, x)


def _gmlp_kernel(x_ref, g_ref, w_in_ref, lng_ref, lnb_ref, ws_ref, bs_ref, w_out_ref, pg_ref,
                 o_ref, *gv_ref, tm, single_row_chunks):
    d = D_MODEL
    x = x_ref[...]
    hb = _rms(x, g_ref[...], NORM_EPS).astype(BF16)
    u = _gelu(_dot(hb, w_in_ref[:, 0:d]))
    v = _gelu(_dot(hb, w_in_ref[:, d:2 * d]))
    mu = jnp.mean(v, axis=-1, keepdims=True)
    vc = v - mu
    v = vc * lax.rsqrt(jnp.mean(vc * vc, axis=-1, keepdims=True) + LN_EPS) * lng_ref[...] + lnb_ref[...]
    gate = _silu(_dot(hb, w_in_ref[:, 2 * d:3 * d]))
    if single_row_chunks:
        s = v * ws_ref[...] + bs_ref[...]
        gv_ref[0][...] = v
    else:
        r = lax.broadcasted_iota(jnp.int32, (CHUNK, CHUNK), 0)
        c = lax.broadcasted_iota(jnp.int32, (CHUNK, CHUNK), 1)
        vb = v.astype(BF16)
        rows = []
        for ci in range(tm // CHUNK):
            cols = []
            for gi in range(N_GROUPS):
                w = jnp.where(c <= r, ws_ref[gi], 0.0).astype(BF16)
                cols.append(_dot(w, vb[ci * CHUNK:(ci + 1) * CHUNK, gi * GROUP_DIM:(gi + 1) * GROUP_DIM]))
            rows.append(jnp.concatenate(cols, axis=-1) + bs_ref[...])
        s = jnp.concatenate(rows, axis=0)
    z = (u * s * gate).astype(BF16)
    o_ref[...] = x + _rms(_dot(z, w_out_ref[...]), pg_ref[...], NORM_EPS)


def _gmlp(x, pre_g, w_in16, ln_g, ln_b, ws, bs, w_out16, post_g, tm, single_row_chunks):
    m = x.shape[0]
    row = lambda i: (i, 0)
    const = lambda i: (0, 0)
    vec = pl.BlockSpec((1, D_MODEL), const)
    if single_row_chunks:
        ws_spec = vec
        bs_spec = vec
        out_shape = [jax.ShapeDtypeStruct((m, D_MODEL), F32)] * 2
        out_specs = [pl.BlockSpec((tm, D_MODEL), row)] * 2
    else:
        ws_spec = pl.BlockSpec((N_GROUPS, CHUNK, CHUNK), lambda i: (0, 0, 0))
        bs_spec = pl.BlockSpec((CHUNK, D_MODEL), const)
        out_shape = jax.ShapeDtypeStruct((m, D_MODEL), F32)
        out_specs = pl.BlockSpec((tm, D_MODEL), row)
    return pl.pallas_call(
        functools.partial(_gmlp_kernel, tm=tm, single_row_chunks=single_row_chunks),
        grid=(m // tm,),
        in_specs=[
            pl.BlockSpec((tm, D_MODEL), row),
            vec,
            pl.BlockSpec((D_MODEL, 3 * D_MODEL), const),
            vec, vec, ws_spec, bs_spec,
            pl.BlockSpec((D_MODEL, D_MODEL), const),
            vec,
        ],
        out_specs=out_specs,
        out_shape=out_shape,
        compiler_params=pltpu.CompilerParams(
            dimension_semantics=("arbitrary",), vmem_limit_bytes=VMEM_LIMIT_BYTES),
        name="gmlp_rows" if single_row_chunks else "gmlp_chunks",
    )(x, pre_g, w_in16, ln_g, ln_b, ws, bs, w_out16, post_g)


PROMPT_TM = 256
FLASH_TQ = 512
PAGES_PER_STEP = 8


def kernel(x_prompt, x_sample, cache_k_l0, cache_v_l0, cache_k_l2, cache_v_l2, page_table, pre_norm_g, post_norm_g, attn_w_in, attn_lambda_q1, attn_lambda_k1, attn_lambda_q2, attn_lambda_k2, attn_subln_g, attn_w_out, gmlp_w_in, gmlp_ln_g, gmlp_ln_b, gmlp_w_s, gmlp_b_s, gmlp_w_out):
    batch, seq, d = x_prompt.shape
    db, dec_seq, _ = x_sample.shape
    assert dec_seq == 1 and d == D_MODEL
    past = page_table.shape[1] * PAGE_SIZE
    depth = pre_norm_g.shape[0]
    caches = ((cache_k_l0, cache_v_l0), (cache_k_l2, cache_v_l2))

    cos_p, sin_p = _rope_tables(jnp.arange(seq))
    cos_s, sin_s = _rope_tables(jnp.full((db,), past))

    xp = x_prompt.reshape(batch * seq, d)
    xs = x_sample.reshape(db, d)
    vec = lambda a: a.reshape(1, -1)
    kp_out, vp_out, ks_out, vs_out, gv_out = [], [], [], [], []
    for i in range(depth):
        li = i // 2
        pre_g = vec(pre_norm_g[i])
        post_g = vec(post_norm_g[i])
        if i % 2 == 0:
            lam0 = _lambda_init(i)
            lams = [vec(a[li]) for a in (attn_lambda_q1, attn_lambda_k1, attn_lambda_q2, attn_lambda_k2)]
            w_in16 = attn_w_in[li].astype(BF16)
            w_out16 = attn_w_out[li].astype(BF16)
            subln = vec(attn_subln_g[li])
            q16, kt32, k16, v32, vt16, gate16 = _attn_in(xp, pre_g, w_in16, cos_p, sin_p, PROMPT_TM,
                                                         kv_chunk=FLASH_TQ)
            og = _flash(q16, k16, vt16, gate16, lams, subln, lam0, batch, seq, FLASH_TQ)
            xp = _out_proj(og, w_out16, post_g, xp, PROMPT_TM)
            kp_out.append(jnp.transpose(kt32.reshape(batch, N_QK_HEADS, HEAD_DIM, seq), (0, 3, 1, 2)))
            vp_out.append(v32.reshape(batch, seq, N_HEADS, V_HEAD_DIM))

            qs32, ks32, vs32, gs32 = _attn_in(xs, pre_g, w_in16, cos_s, sin_s, db)
            ck, cv = caches[li]
            ogs = _decode_attn(qs32, ks32, vs32, gs32, ck, cv, page_table, lams, subln, lam0,
                               PAGES_PER_STEP)
            xs = _out_proj(ogs, w_out16, post_g, xs, db)
            ks_out.append(ks32.reshape(db, 1, N_QK_HEADS, HEAD_DIM))
            vs_out.append(vs32.reshape(db, 1, N_HEADS, V_HEAD_DIM))
        else:
            w_in16 = gmlp_w_in[li].astype(BF16)
            w_out16 = gmlp_w_out[li].astype(BF16)
            ln_g = vec(gmlp_ln_g[li])
            ln_b = vec(gmlp_ln_b[li])
            ws = gmlp_w_s[li]
            bs = gmlp_b_s[li]
            bs_full = jnp.repeat(bs.T, GROUP_DIM, axis=1)
            xp = _gmlp(xp, pre_g, w_in16, ln_g, ln_b, ws, bs_full, w_out16, post_g, PROMPT_TM, False)
            ws0 = vec(jnp.repeat(ws[:, 0, 0], GROUP_DIM))
            bs0 = vec(jnp.repeat(bs[:, 0], GROUP_DIM))
            xs, gv = _gmlp(xs, pre_g, w_in16, ln_g, ln_b, ws0, bs0, w_out16, post_g, db, True)
            gv_out.append(gv.reshape(db, 1, d))

    return (xp.reshape(batch, seq, d), xs.reshape(db, 1, d),
            kp_out[0], vp_out[0], ks_out[0], vs_out[0], gv_out[0],
            kp_out[1], vp_out[1], ks_out[1], vs_out[1], gv_out[1])
```

```python
import functools
import math

import jax
import jax.numpy as jnp
from jax import lax
from jax.experimental import pallas as pl
from jax.experimental.pallas import tpu as pltpu

D_MODEL = 1024
N_HEADS = 8
HEAD_DIM = 64
N_QK_HEADS = 16
V_HEAD_DIM = 128
PAGE_SIZE = 128
CHUNK = 128
N_GROUPS = 8
GROUP_DIM = 128
ROPE_THETA = 10000.0
NORM_EPS = 1e-6
SUBLN_EPS = 1e-5
LN_EPS = 1e-5

LANES = 128
VMEM_LIMIT_BYTES = 56 * 1024 * 1024

F32 = jnp.float32
BF16 = jnp.bfloat16


def _lambda_init(layer):
    return 0.8 - 0.6 * math.exp(-0.3 * layer)


def _rms(x, g, eps):
    return x * lax.rsqrt(jnp.mean(x * x, axis=-1, keepdims=True) + eps) * g


def _silu(x):
    return x * (1.0 / (1.0 + jnp.exp(-x)))


def _gelu(x):
    return 0.5 * x * (1.0 + lax.erf(x * (2.0 ** -0.5)))


def _dot(a, b):
    return jnp.dot(a, b, preferred_element_type=F32)


def _dot_nt(a, b):
    return lax.dot_general(a, b, (((1,), (1,)), ((), ())), preferred_element_type=F32)


def _lam(lq1, lk1, lq2, lk2, lam0):
    return (jnp.exp(jnp.sum(lq1[...] * lk1[...], axis=-1, keepdims=True))
            - jnp.exp(jnp.sum(lq2[...] * lk2[...], axis=-1, keepdims=True)) + lam0)


def _rope_tables(positions):
    inv = ROPE_THETA ** (-jnp.arange(0, HEAD_DIM, 2, dtype=F32) / HEAD_DIM)
    ang = positions.astype(F32)[:, None] * inv[None, :]
    cos = jnp.cos(ang)
    sin = jnp.sin(ang)
    cos2 = jnp.concatenate([cos, cos, cos, cos], axis=-1)
    sin2 = jnp.concatenate([-sin, sin, -sin, sin], axis=-1)
    return cos2, sin2


def _rope(t, cos, sin):
    lane = lax.broadcasted_iota(jnp.int32, (t.shape[0], LANES), 1)
    first_half = (lane % HEAD_DIM) < (HEAD_DIM // 2)
    outs = []
    for c in range(t.shape[1] // LANES):
        tc = t[:, c * LANES:(c + 1) * LANES]
        fwd = pltpu.roll(tc, LANES - HEAD_DIM // 2, 1)
        bwd = pltpu.roll(tc, HEAD_DIM // 2, 1)
        rot = jnp.where(first_half, fwd, bwd)
        outs.append(tc * cos + rot * sin)
    return jnp.concatenate(outs, axis=-1)


def _attn_in_kernel(x_ref, g_ref, w_ref, cos_ref, sin_ref, *out_refs, prompt):
    d = D_MODEL
    hb = _rms(x_ref[...], g_ref[...], NORM_EPS).astype(BF16)
    cos = cos_ref[...]
    sin = sin_ref[...]
    q = _rope(_dot(hb, w_ref[:, 0:d]), cos, sin) * (HEAD_DIM ** -0.5)
    k = _rope(_dot(hb, w_ref[:, d:2 * d]), cos, sin)
    v = _dot(hb, w_ref[:, 2 * d:3 * d])
    gate = _silu(_dot(hb, w_ref[:, 3 * d:4 * d]))
    if prompt:
        q_ref, kt32_ref, k16_ref, v32_ref, vt16_ref, gate_ref = out_refs
        q_ref[...] = q.astype(BF16)
        kt32_ref[...] = k.T
        k16_ref[...] = k.astype(BF16)
        v32_ref[...] = v
        vt16_ref[0] = v.T.astype(BF16)
        gate_ref[...] = gate.astype(BF16)
    else:
        q_ref, k32_ref, v32_ref, gate_ref = out_refs
        q_ref[...] = q
        k32_ref[...] = k
        v32_ref[...] = v
        gate_ref[...] = gate


def _attn_in(x, g, w16, cos, sin, tm, kv_chunk=None):
    m = x.shape[0]
    prompt = kv_chunk is not None
    n_pos_blocks = cos.shape[0] // tm
    row = lambda i: (i, 0)
    const = lambda i: (0, 0)
    row_spec = pl.BlockSpec((tm, D_MODEL), row)
    rows = lambda dt: jax.ShapeDtypeStruct((m, D_MODEL), dt)
    if prompt:
        r = kv_chunk // tm
        seq = cos.shape[0]
        out_shape = [rows(BF16), jax.ShapeDtypeStruct((m // seq, D_MODEL, seq), F32), rows(BF16),
                     rows(F32),
                     jax.ShapeDtypeStruct((m // kv_chunk, D_MODEL, kv_chunk), BF16), rows(BF16)]
        out_specs = [row_spec,
                     pl.BlockSpec((None, D_MODEL, tm),
                                  lambda i: (i // n_pos_blocks, 0, i % n_pos_blocks)),
                     row_spec, row_spec,
                     pl.BlockSpec((1, D_MODEL, tm), lambda i: (i // r, 0, i % r)), row_spec]
    else:
        out_shape = [rows(F32)] * 4
        out_specs = [row_spec] * 4
    return pl.pallas_call(
        functools.partial(_attn_in_kernel, prompt=prompt),
        grid=(m // tm,),
        in_specs=[
            row_spec,
            pl.BlockSpec((1, D_MODEL), const),
            pl.BlockSpec((D_MODEL, 4 * D_MODEL), const),
            pl.BlockSpec((tm, LANES), lambda i: (i % n_pos_blocks, 0)),
            pl.BlockSpec((tm, LANES), lambda i: (i % n_pos_blocks, 0)),
        ],
        out_specs=out_specs,
        out_shape=out_shape,
        compiler_params=pltpu.CompilerParams(
            dimension_semantics=("arbitrary",), vmem_limit_bytes=VMEM_LIMIT_BYTES),
        name="attn_in_prompt" if prompt else "attn_in_decode",
    )(x, g, w16, cos, sin)


ONES_ROWS = 16


def _flash_kernel(q_ref, k_ref, vt_ref, gate_ref, lq1, lk1, lq2, lk2, sg_ref, o_ref,
                  q2_s, sa_s, sb_s, m_s, acc_s, *, tq, tk, lam0):
    assert tq == 2 * tk
    qi = pl.program_id(2)
    q = q_ref[...].astype(F32)
    lane = lax.broadcasted_iota(jnp.int32, (tq, LANES), 1)
    zero = jnp.zeros_like(q)
    q2_s[0:tq, :] = jnp.where(lane < HEAD_DIM, q, zero).astype(BF16)
    q2_s[tq:2 * tq, :] = jnp.where(lane < HEAD_DIM, zero, q).astype(BF16)
    m_s[...] = jnp.full(m_s.shape, -jnp.inf, F32)
    acc_s[...] = jnp.zeros(acc_s.shape, F32)
    ones = jnp.ones((ONES_ROWS, tk), BF16)

    def scores(ki, dst):
        start = pl.multiple_of(ki * tk, tk)
        dst[...] = _dot_nt(k_ref[pl.ds(start, tk), :], q2_s[...])

    def consume(src, ki, key_offset=None):
        st = src[...]
        if key_offset is not None:
            key = lax.broadcasted_iota(jnp.int32, (tk, 2 * tq), 0) + key_offset
            col = lax.broadcasted_iota(jnp.int32, (tk, 2 * tq), 1)
            qpos = jnp.where(col >= tq, col - tq, col)
            st = jnp.where(key <= qpos, st, -jnp.inf)
        m_old = m_s[...]
        m_new = jnp.maximum(m_old, jnp.max(st, axis=0, keepdims=True))
        alpha = jnp.exp(m_old - m_new)
        pt = jnp.exp(st - m_new).astype(BF16)
        vt1 = jnp.concatenate([vt_ref[ki], ones], axis=0)
        acc_s[...] = alpha * acc_s[...] + _dot(vt1, pt)
        m_s[...] = m_new

    scores(0, sa_s)

    def pair(j, carry):
        scores(2 * j + 1, sb_s)
        consume(sa_s, 2 * j)
        scores(2 * j + 2, sa_s)
        consume(sb_s, 2 * j + 1)
        return carry

    lax.fori_loop(0, qi, pair, 0)
    scores(2 * qi + 1, sb_s)
    consume(sa_s, 2 * qi, key_offset=0)
    consume(sb_s, 2 * qi + 1, key_offset=tk)

    lam = _lam(lq1, lk1, lq2, lk2, lam0)
    accn = acc_s[0:V_HEAD_DIM, :] / acc_s[V_HEAD_DIM:V_HEAD_DIM + 1, :]
    ot = accn[:, 0:tq] - lam * accn[:, tq:2 * tq]
    o = _rms(ot.T, sg_ref[...], SUBLN_EPS) * (1.0 - lam0)
    o_ref[...] = (o * gate_ref[...].astype(F32)).astype(BF16)


def _flash(q16, k16, vt16, gate16, lams, subln_g, lam0, batch, seq, tq, tk):
    m = q16.shape[0]
    nq = seq // tq
    assert vt16.shape == (m // tk, D_MODEL, tk)
    qmap = lambda b, h, i: (b * nq + i, h)
    small = lambda b, h, i: (0, 0)
    lam_specs = [pl.BlockSpec((1, HEAD_DIM), small)] * 4
    return pl.pallas_call(
        functools.partial(_flash_kernel, tq=tq, tk=tk, lam0=lam0),
        grid=(batch, N_HEADS, nq),
        in_specs=[
            pl.BlockSpec((tq, LANES), qmap),
            pl.BlockSpec((seq, LANES), lambda b, h, i: (b, h)),
            pl.BlockSpec((seq // tk, V_HEAD_DIM, tk), lambda b, h, i: (b, h, 0)),
            pl.BlockSpec((tq, LANES), qmap),
            *lam_specs,
            pl.BlockSpec((1, V_HEAD_DIM), small),
        ],
        out_specs=pl.BlockSpec((tq, LANES), qmap),
        out_shape=jax.ShapeDtypeStruct((m, D_MODEL), BF16),
        scratch_shapes=[
            pltpu.VMEM((2 * tq, LANES), BF16),
            pltpu.VMEM((tk, 2 * tq), F32),
            pltpu.VMEM((tk, 2 * tq), F32),
            pltpu.VMEM((1, 2 * tq), F32),
            pltpu.VMEM((V_HEAD_DIM + ONES_ROWS, 2 * tq), F32),
        ],
        compiler_params=pltpu.CompilerParams(
            dimension_semantics=("arbitrary", "arbitrary", "arbitrary"),
            vmem_limit_bytes=VMEM_LIMIT_BYTES),
        name="flash_diff_attn",
    )(q16, k16, vt16, gate16, *lams, subln_g)


def _decode_kernel(pt_ref, qb_ref, knb_ref, vn_ref, gate_ref, lq1, lk1, lq2, lk2, sg_ref, *rest,
                   pages_per_step, lam0):
    del pt_ref
    np_ = pages_per_step
    k_refs = rest[:np_]
    v_refs = rest[np_:2 * np_]
    o_ref, m_s, l_s, acc_s = rest[2 * np_:]
    step = pl.program_id(1)
    n_steps = pl.num_programs(1)
    qb = qb_ref[...]

    @pl.when(step == 0)
    def _():
        m_s[...] = jnp.sum(qb * knb_ref[...], axis=1)[:, 0:1]
        l_s[...] = jnp.ones(l_s.shape, F32)
        acc_s[...] = vn_ref[...]

    s = jnp.concatenate([jnp.sum(qb * k_refs[j][...], axis=1) for j in range(np_)],
                        axis=-1)
    m_old = m_s[...]
    m_new = jnp.maximum(m_old, jnp.max(s, axis=-1, keepdims=True))
    alpha = jnp.exp(m_old - m_new)
    p = jnp.exp(s - m_new)
    l_s[...] = alpha * l_s[...] + jnp.sum(p, axis=-1, keepdims=True)
    row = lax.broadcasted_iota(jnp.int32, p.shape, 0)
    pv = None
    for hv in range(N_HEADS):
        ph = jnp.where(row // 2 == hv, p, 0.0).astype(BF16)
        vh = jnp.concatenate([r[pl.ds(hv, PAGE_SIZE, stride=N_HEADS), :] for r in v_refs],
                             axis=0).astype(BF16)
        d = _dot(ph, vh)
        pv = d if pv is None else pv + d
    acc_s[...] = alpha * acc_s[...] + pv
    m_s[...] = m_new

    @pl.when(step == n_steps - 1)
    def _():
        lam = _lam(lq1, lk1, lq2, lk2, lam0)
        acc_s[...] = acc_s[...] / l_s[...]
        even = acc_s[pl.ds(0, N_HEADS, stride=2), :]
        odd = acc_s[pl.ds(1, N_HEADS, stride=2), :]
        o = _rms(even - lam * odd, sg_ref[...], SUBLN_EPS) * (1.0 - lam0)
        o_ref[...] = o * gate_ref[...]


def _decode_attn(q32, k_new, v_new, gate32, cache_k, cache_v, page_table, lams, subln_g, lam0,
                 pages_per_step):
    db = q32.shape[0]
    n_pages = page_table.shape[1]
    kt = jnp.transpose(cache_k, (0, 2, 3, 1))
    lane_bcast = lambda a: jnp.broadcast_to(
        a.reshape(db, N_QK_HEADS, HEAD_DIM, 1), (db, N_QK_HEADS, HEAD_DIM, LANES))
    vn = jnp.repeat(v_new.reshape(db, N_HEADS, V_HEAD_DIM), 2, axis=1)
    tok4 = pl.BlockSpec((None, N_QK_HEADS, HEAD_DIM, LANES), lambda b, s, pt: (b, 0, 0, 0))
    small = lambda b, s, pt: (0, 0)

    vr = cache_v.reshape(cache_v.shape[0], PAGE_SIZE * N_HEADS, V_HEAD_DIM)

    def kpage(j):
        return lambda b, s, pt: (pt[b, s * pages_per_step + j], 0, 0, 0)

    def vpage(j):
        return lambda b, s, pt: (pt[b, s * pages_per_step + j], 0, 0)

    k_specs = [pl.BlockSpec((None, N_QK_HEADS, HEAD_DIM, PAGE_SIZE), kpage(j))
               for j in range(pages_per_step)]
    v_specs = [pl.BlockSpec((None, PAGE_SIZE * N_HEADS, V_HEAD_DIM), vpage(j))
               for j in range(pages_per_step)]
    grid_spec = pltpu.PrefetchScalarGridSpec(
        num_scalar_prefetch=1,
        grid=(db, n_pages // pages_per_step),
        in_specs=[tok4, tok4,
                  pl.BlockSpec((None, N_QK_HEADS, V_HEAD_DIM), lambda b, s, pt: (b, 0, 0)),
                  pl.BlockSpec((None, N_HEADS, V_HEAD_DIM), lambda b, s, pt: (b, 0, 0)),
                  *[pl.BlockSpec((1, HEAD_DIM), small)] * 4,
                  pl.BlockSpec((1, V_HEAD_DIM), small),
                  *k_specs, *v_specs],
        out_specs=pl.BlockSpec((None, N_HEADS, V_HEAD_DIM), lambda b, s, pt: (b, 0, 0)),
        scratch_shapes=[
            pltpu.VMEM((N_QK_HEADS, 1), F32),
            pltpu.VMEM((N_QK_HEADS, 1), F32),
            pltpu.VMEM((N_QK_HEADS, V_HEAD_DIM), F32),
        ],
    )
    out = pl.pallas_call(
        functools.partial(_decode_kernel, pages_per_step=pages_per_step, lam0=lam0),
        grid_spec=grid_spec,
        out_shape=jax.ShapeDtypeStruct((db, N_HEADS, V_HEAD_DIM), F32),
        compiler_params=pltpu.CompilerParams(
            dimension_semantics=("arbitrary", "arbitrary"),
            vmem_limit_bytes=VMEM_LIMIT_BYTES),
        name="paged_diff_attn",
    )(page_table, lane_bcast(q32), lane_bcast(k_new), vn,
      gate32.reshape(db, N_HEADS, V_HEAD_DIM), *lams, subln_g,
      *([kt] * pages_per_step), *([vr] * pages_per_step))
    return out.reshape(db, D_MODEL)


def _out_proj_kernel(a_ref, w_ref, pg_ref, x_ref, o_ref):
    y = _dot(a_ref[...].astype(BF16), w_ref[...])
    o_ref[...] = x_ref[...] + _rms(y, pg_ref[...], NORM_EPS)


def _out_proj(a, w16, post_g, x, tm):
    m = x.shape[0]
    row = lambda i: (i, 0)
    const = lambda i: (0, 0)
    return pl.pallas_call(
        _out_proj_kernel,
        grid=(m // tm,),
        in_specs=[
            pl.BlockSpec((tm, D_MODEL), row),
            pl.BlockSpec((D_MODEL, D_MODEL), const),
            pl.BlockSpec((1, D_MODEL), const),
            pl.BlockSpec((tm, D_MODEL), row),
        ],
        out_specs=pl.BlockSpec((tm, D_MODEL), row),
        out_shape=jax.ShapeDtypeStruct((m, D_MODEL), F32),
        compiler_params=pltpu.CompilerParams(
            dimension_semantics=("arbitrary",), vmem_limit_bytes=VMEM_LIMIT_BYTES),
        name="out_proj",
    )(a, w16, post_g, x)


def _gmlp_kernel(*refs, tm, single_row_chunks, prev_out_proj):
    if prev_out_proj:
        a_ref, wo_ref, pgo_ref = refs[:3]
        refs = refs[3:]
    x_ref, g_ref, w_in_ref, lng_ref, lnb_ref, ws_ref, bs_ref, w_out_ref, pg_ref, o_ref, *gv_ref = refs
    d = D_MODEL
    x = x_ref[...]
    if prev_out_proj:
        x = x + _rms(_dot(a_ref[...], wo_ref[...]), pgo_ref[...], NORM_EPS)
    hb = _rms(x, g_ref[...], NORM_EPS).astype(BF16)
    u = _gelu(_dot(hb, w_in_ref[:, 0:d]))
    v = _gelu(_dot(hb, w_in_ref[:, d:2 * d]))
    mu = jnp.mean(v, axis=-1, keepdims=True)
    vc = v - mu
    v = vc * lax.rsqrt(jnp.mean(vc * vc, axis=-1, keepdims=True) + LN_EPS) * lng_ref[...] + lnb_ref[...]
    gate = _silu(_dot(hb, w_in_ref[:, 2 * d:3 * d]))
    if single_row_chunks:
        s = v * ws_ref[...] + bs_ref[...]
        gv_ref[0][...] = v
    else:
        r = lax.broadcasted_iota(jnp.int32, (CHUNK, CHUNK), 0)
        c = lax.broadcasted_iota(jnp.int32, (CHUNK, CHUNK), 1)
        vb = v.astype(BF16)
        rows = []
        for ci in range(tm // CHUNK):
            cols = []
            for gi in range(N_GROUPS):
                w = jnp.where(c <= r, ws_ref[gi], 0.0).astype(BF16)
                cols.append(_dot(w, vb[ci * CHUNK:(ci + 1) * CHUNK, gi * GROUP_DIM:(gi + 1) * GROUP_DIM]))
            rows.append(jnp.concatenate(cols, axis=-1) + bs_ref[...])
        s = jnp.concatenate(rows, axis=0)
    z = (u * s * gate).astype(BF16)
    o_ref[...] = x + _rms(_dot(z, w_out_ref[...]), pg_ref[...], NORM_EPS)


def _gmlp(x, pre_g, w_in16, ln_g, ln_b, ws, bs, w_out16, post_g, tm, single_row_chunks,
          prev_out_proj=None):
    m = x.shape[0]
    row = lambda i: (i, 0)
    const = lambda i: (0, 0)
    vec = pl.BlockSpec((1, D_MODEL), const)
    sq = pl.BlockSpec((D_MODEL, D_MODEL), const)
    prev_specs = [pl.BlockSpec((tm, D_MODEL), row), sq, vec] if prev_out_proj else []
    prev_args = list(prev_out_proj) if prev_out_proj else []
    if single_row_chunks:
        ws_spec = vec
        bs_spec = vec
        out_shape = [jax.ShapeDtypeStruct((m, D_MODEL), F32)] * 2
        out_specs = [pl.BlockSpec((tm, D_MODEL), row)] * 2
    else:
        ws_spec = pl.BlockSpec((N_GROUPS, CHUNK, CHUNK), lambda i: (0, 0, 0))
        bs_spec = pl.BlockSpec((CHUNK, D_MODEL), const)
        out_shape = jax.ShapeDtypeStruct((m, D_MODEL), F32)
        out_specs = pl.BlockSpec((tm, D_MODEL), row)
    return pl.pallas_call(
        functools.partial(_gmlp_kernel, tm=tm, single_row_chunks=single_row_chunks,
                          prev_out_proj=bool(prev_out_proj)),
        grid=(m // tm,),
        in_specs=[
            *prev_specs,
            pl.BlockSpec((tm, D_MODEL), row),
            vec,
            pl.BlockSpec((D_MODEL, 3 * D_MODEL), const),
            vec, vec, ws_spec, bs_spec,
            sq,
            vec,
        ],
        out_specs=out_specs,
        out_shape=out_shape,
        compiler_params=pltpu.CompilerParams(
            dimension_semantics=("arbitrary",), vmem_limit_bytes=VMEM_LIMIT_BYTES),
        name="gmlp_rows" if single_row_chunks else "gmlp_chunks",
    )(*prev_args, x, pre_g, w_in16, ln_g, ln_b, ws, bs, w_out16, post_g)


PROMPT_TM = 256
FLASH_TK = 512
FLASH_TQ = 2 * FLASH_TK
PAGES_PER_STEP = 16


def kernel(x_prompt, x_sample, cache_k_l0, cache_v_l0, cache_k_l2, cache_v_l2, page_table, pre_norm_g, post_norm_g, attn_w_in, attn_lambda_q1, attn_lambda_k1, attn_lambda_q2, attn_lambda_k2, attn_subln_g, attn_w_out, gmlp_w_in, gmlp_ln_g, gmlp_ln_b, gmlp_w_s, gmlp_b_s, gmlp_w_out):
    batch, seq, d = x_prompt.shape
    db, dec_seq, _ = x_sample.shape
    assert dec_seq == 1 and d == D_MODEL
    past = page_table.shape[1] * PAGE_SIZE
    depth = pre_norm_g.shape[0]
    caches = ((cache_k_l0, cache_v_l0), (cache_k_l2, cache_v_l2))

    cos_p, sin_p = _rope_tables(jnp.arange(seq))
    cos_s, sin_s = _rope_tables(jnp.full((db,), past))

    xp = x_prompt.reshape(batch * seq, d)
    xs = x_sample.reshape(db, d)
    vec = lambda a: a.reshape(1, -1)
    kp_out, vp_out, ks_out, vs_out, gv_out = [], [], [], [], []
    pending = None
    for i in range(depth):
        li = i // 2
        pre_g = vec(pre_norm_g[i])
        post_g = vec(post_norm_g[i])
        if i % 2 == 0:
            lam0 = _lambda_init(i)
            lams = [vec(a[li]) for a in (attn_lambda_q1, attn_lambda_k1, attn_lambda_q2, attn_lambda_k2)]
            w_in16 = attn_w_in[li].astype(BF16)
            w_out16 = attn_w_out[li].astype(BF16)
            subln = vec(attn_subln_g[li])
            q16, kt32, k16, v32, vt16, gate16 = _attn_in(xp, pre_g, w_in16, cos_p, sin_p, PROMPT_TM,
                                                         kv_chunk=FLASH_TK)
            og = _flash(q16, k16, vt16, gate16, lams, subln, lam0, batch, seq, FLASH_TQ, FLASH_TK)
            if i + 1 < depth:
                pending = (og, w_out16, post_g)
            else:
                xp = _out_proj(og, w_out16, post_g, xp, PROMPT_TM)
            kp_out.append(jnp.transpose(kt32.reshape(batch, N_QK_HEADS, HEAD_DIM, seq), (0, 3, 1, 2)))
            vp_out.append(v32.reshape(batch, seq, N_HEADS, V_HEAD_DIM))

            qs32, ks32, vs32, gs32 = _attn_in(xs, pre_g, w_in16, cos_s, sin_s, db)
            ck, cv = caches[li]
            ogs = _decode_attn(qs32, ks32, vs32, gs32, ck, cv, page_table, lams, subln, lam0,
                               PAGES_PER_STEP)
            xs = _out_proj(ogs, w_out16, post_g, xs, db)
            ks_out.append(ks32.reshape(db, 1, N_QK_HEADS, HEAD_DIM))
            vs_out.append(vs32.reshape(db, 1, N_HEADS, V_HEAD_DIM))
        else:
            w_in16 = gmlp_w_in[li].astype(BF16)
            w_out16 = gmlp_w_out[li].astype(BF16)
            ln_g = vec(gmlp_ln_g[li])
            ln_b = vec(gmlp_ln_b[li])
            ws = gmlp_w_s[li]
            bs = gmlp_b_s[li]
            bs_full = jnp.repeat(bs.T, GROUP_DIM, axis=1)
            xp = _gmlp(xp, pre_g, w_in16, ln_g, ln_b, ws, bs_full, w_out16, post_g, PROMPT_TM, False,
                       prev_out_proj=pending)
            pending = None
            ws0 = vec(jnp.repeat(ws[:, 0, 0], GROUP_DIM))
            bs0 = vec(jnp.repeat(bs[:, 0], GROUP_DIM))
            xs, gv = _gmlp(xs, pre_g, w_in16, ln_g, ln_b, ws0, bs0, w_out16, post_g, db, True)
            gv_out.append(gv.reshape(db, 1, d))

    return (xp.reshape(batch, seq, d), xs.reshape(db, 1, d),
            kp_out[0], vp_out[0], ks_out[0], vs_out[0], gv_out[0],
            kp_out[1], vp_out[1], ks_out[1], vs_out[1], gv_out[1])
```

```python
import functools
import math

import jax
import jax.numpy as jnp
from jax import lax
from jax.experimental import pallas as pl
from jax.experimental.pallas import tpu as pltpu

D_MODEL = 1024
N_HEADS = 8
HEAD_DIM = 64
N_QK_HEADS = 16
V_HEAD_DIM = 128
PAGE_SIZE = 128
CHUNK = 128
N_GROUPS = 8
GROUP_DIM = 128
ROPE_THETA = 10000.0
NORM_EPS = 1e-6
SUBLN_EPS = 1e-5
LN_EPS = 1e-5

LANES = 128
VMEM_LIMIT_BYTES = 56 * 1024 * 1024

F32 = jnp.float32
BF16 = jnp.bfloat16


def _lambda_init(layer):
    return 0.8 - 0.6 * math.exp(-0.3 * layer)


def _rms(x, g, eps):
    return x * lax.rsqrt(jnp.mean(x * x, axis=-1, keepdims=True) + eps) * g


def _silu(x):
    return x * (1.0 / (1.0 + jnp.exp(-x)))


def _gelu(x):
    return 0.5 * x * (1.0 + lax.erf(x * (2.0 ** -0.5)))


def _dot(a, b):
    return jnp.dot(a, b, preferred_element_type=F32)


def _dot_nt(a, b):
    return lax.dot_general(a, b, (((1,), (1,)), ((), ())), preferred_element_type=F32)


def _lam(lq1, lk1, lq2, lk2, lam0):
    return (jnp.exp(jnp.sum(lq1[...] * lk1[...], axis=-1, keepdims=True))
            - jnp.exp(jnp.sum(lq2[...] * lk2[...], axis=-1, keepdims=True)) + lam0)


def _rope_tables(positions):
    inv = ROPE_THETA ** (-jnp.arange(0, HEAD_DIM, 2, dtype=F32) / HEAD_DIM)
    ang = positions.astype(F32)[:, None] * inv[None, :]
    cos = jnp.cos(ang)
    sin = jnp.sin(ang)
    cos2 = jnp.concatenate([cos, cos, cos, cos], axis=-1)
    sin2 = jnp.concatenate([-sin, sin, -sin, sin], axis=-1)
    return cos2, sin2


def _rope(t, cos, sin):
    lane = lax.broadcasted_iota(jnp.int32, (t.shape[0], LANES), 1)
    first_half = (lane % HEAD_DIM) < (HEAD_DIM // 2)
    outs = []
    for c in range(t.shape[1] // LANES):
        tc = t[:, c * LANES:(c + 1) * LANES]
        fwd = pltpu.roll(tc, LANES - HEAD_DIM // 2, 1)
        bwd = pltpu.roll(tc, HEAD_DIM // 2, 1)
        rot = jnp.where(first_half, fwd, bwd)
        outs.append(tc * cos + rot * sin)
    return jnp.concatenate(outs, axis=-1)


def _attn_in_kernel(x_ref, g_ref, w_ref, cos_ref, sin_ref, *out_refs, prompt):
    d = D_MODEL
    hb = _rms(x_ref[...], g_ref[...], NORM_EPS).astype(BF16)
    cos = cos_ref[...]
    sin = sin_ref[...]
    q = _rope(_dot(hb, w_ref[:, 0:d]), cos, sin) * (HEAD_DIM ** -0.5)
    k = _rope(_dot(hb, w_ref[:, d:2 * d]), cos, sin)
    v = _dot(hb, w_ref[:, 2 * d:3 * d])
    gate = _silu(_dot(hb, w_ref[:, 3 * d:4 * d]))
    if prompt:
        q_ref, kt32_ref, k16_ref, v32_ref, vt16_ref, gate_ref = out_refs
        q_ref[...] = q.astype(BF16)
        kt32_ref[...] = k.T
        k16_ref[...] = k.astype(BF16)
        v32_ref[...] = v
        vt16_ref[0] = v.T.astype(BF16)
        gate_ref[...] = gate.astype(BF16)
    else:
        q_ref, k32_ref, v32_ref, gate_ref = out_refs
        q_ref[...] = q
        k32_ref[...] = k
        v32_ref[...] = v
        gate_ref[...] = gate


def _attn_in(x, g, w16, cos, sin, tm, kv_chunk=None):
    m = x.shape[0]
    prompt = kv_chunk is not None
    n_pos_blocks = cos.shape[0] // tm
    row = lambda i: (i, 0)
    const = lambda i: (0, 0)
    row_spec = pl.BlockSpec((tm, D_MODEL), row)
    rows = lambda dt: jax.ShapeDtypeStruct((m, D_MODEL), dt)
    if prompt:
        r = kv_chunk // tm
        seq = cos.shape[0]
        out_shape = [rows(BF16), jax.ShapeDtypeStruct((m // seq, D_MODEL, seq), F32), rows(BF16),
                     rows(F32),
                     jax.ShapeDtypeStruct((m // kv_chunk, D_MODEL, kv_chunk), BF16), rows(BF16)]
        out_specs = [row_spec,
                     pl.BlockSpec((None, D_MODEL, tm),
                                  lambda i: (i // n_pos_blocks, 0, i % n_pos_blocks)),
                     row_spec, row_spec,
                     pl.BlockSpec((1, D_MODEL, tm), lambda i: (i // r, 0, i % r)), row_spec]
    else:
        out_shape = [rows(F32)] * 4
        out_specs = [row_spec] * 4
    return pl.pallas_call(
        functools.partial(_attn_in_kernel, prompt=prompt),
        grid=(m // tm,),
        in_specs=[
            row_spec,
            pl.BlockSpec((1, D_MODEL), const),
            pl.BlockSpec((D_MODEL, 4 * D_MODEL), const),
            pl.BlockSpec((tm, LANES), lambda i: (i % n_pos_blocks, 0)),
            pl.BlockSpec((tm, LANES), lambda i: (i % n_pos_blocks, 0)),
        ],
        out_specs=out_specs,
        out_shape=out_shape,
        compiler_params=pltpu.CompilerParams(
            dimension_semantics=("arbitrary",), vmem_limit_bytes=VMEM_LIMIT_BYTES),
        name="attn_in_prompt" if prompt else "attn_in_decode",
    )(x, g, w16, cos, sin)


ONES_ROWS = 16


def _flash_kernel(q_ref, k_ref, vt_ref, gate_ref, lq1, lk1, lq2, lk2, sg_ref, o_ref,
                  q2_s, sa_s, sb_s, m_s, acc_s, *, tq, tk, lam0):
    assert tq == 2 * tk
    qi = pl.program_id(2)
    q = q_ref[...].astype(F32)
    lane = lax.broadcasted_iota(jnp.int32, (tq, LANES), 1)
    zero = jnp.zeros_like(q)
    q2_s[0:tq, :] = jnp.where(lane < HEAD_DIM, q, zero).astype(BF16)
    q2_s[tq:2 * tq, :] = jnp.where(lane < HEAD_DIM, zero, q).astype(BF16)
    m_s[...] = jnp.full(m_s.shape, -jnp.inf, F32)
    acc_s[...] = jnp.zeros(acc_s.shape, F32)
    ones = jnp.ones((ONES_ROWS, tk), BF16)

    def scores(ki, dst):
        start = pl.multiple_of(ki * tk, tk)
        dst[...] = _dot_nt(k_ref[pl.ds(start, tk), :], q2_s[...])

    def consume(src, ki, key_offset=None):
        st = src[...]
        if key_offset is not None:
            key = lax.broadcasted_iota(jnp.int32, (tk, 2 * tq), 0) + key_offset
            col = lax.broadcasted_iota(jnp.int32, (tk, 2 * tq), 1)
            qpos = jnp.where(col >= tq, col - tq, col)
            st = jnp.where(key <= qpos, st, -jnp.inf)
        m_old = m_s[...]
        m_new = jnp.maximum(m_old, jnp.max(st, axis=0, keepdims=True))
        alpha = jnp.exp(m_old - m_new)
        pt = jnp.exp(st - m_new).astype(BF16)
        vt1 = jnp.concatenate([vt_ref[ki], ones], axis=0)
        acc_s[...] = alpha * acc_s[...] + _dot(vt1, pt)
        m_s[...] = m_new

    scores(0, sa_s)

    def pair(j, carry):
        scores(2 * j + 1, sb_s)
        consume(sa_s, 2 * j)
        scores(2 * j + 2, sa_s)
        consume(sb_s, 2 * j + 1)
        return carry

    lax.fori_loop(0, qi, pair, 0)
    scores(2 * qi + 1, sb_s)
    consume(sa_s, 2 * qi, key_offset=0)
    consume(sb_s, 2 * qi + 1, key_offset=tk)

    lam = _lam(lq1, lk1, lq2, lk2, lam0)
    accn = acc_s[0:V_HEAD_DIM, :] / acc_s[V_HEAD_DIM:V_HEAD_DIM + 1, :]
    ot = accn[:, 0:tq] - lam * accn[:, tq:2 * tq]
    o = _rms(ot.T, sg_ref[...], SUBLN_EPS) * (1.0 - lam0)
    o_ref[...] = (o * gate_ref[...].astype(F32)).astype(BF16)


def _flash(q16, k16, vt16, gate16, lams, subln_g, lam0, batch, seq, tq, tk):
    m = q16.shape[0]
    nq = seq // tq
    assert vt16.shape == (m // tk, D_MODEL, tk)
    qmap = lambda b, h, i: (b * nq + i, h)
    small = lambda b, h, i: (0, 0)
    lam_specs = [pl.BlockSpec((1, HEAD_DIM), small)] * 4
    return pl.pallas_call(
        functools.partial(_flash_kernel, tq=tq, tk=tk, lam0=lam0),
        grid=(batch, N_HEADS, nq),
        in_specs=[
            pl.BlockSpec((tq, LANES), qmap),
            pl.BlockSpec((seq, LANES), lambda b, h, i: (b, h)),
            pl.BlockSpec((seq // tk, V_HEAD_DIM, tk), lambda b, h, i: (b, h, 0)),
            pl.BlockSpec((tq, LANES), qmap),
            *lam_specs,
            pl.BlockSpec((1, V_HEAD_DIM), small),
        ],
        out_specs=pl.BlockSpec((tq, LANES), qmap),
        out_shape=jax.ShapeDtypeStruct((m, D_MODEL), BF16),
        scratch_shapes=[
            pltpu.VMEM((2 * tq, LANES), BF16),
            pltpu.VMEM((tk, 2 * tq), F32),
            pltpu.VMEM((tk, 2 * tq), F32),
            pltpu.VMEM((1, 2 * tq), F32),
            pltpu.VMEM((V_HEAD_DIM + ONES_ROWS, 2 * tq), F32),
        ],
        compiler_params=pltpu.CompilerParams(
            dimension_semantics=("arbitrary", "arbitrary", "arbitrary"),
            vmem_limit_bytes=VMEM_LIMIT_BYTES),
        name="flash_diff_attn",
    )(q16, k16, vt16, gate16, *lams, subln_g)


def _decode_kernel(pt_ref, qb_ref, qn_ref, kn_ref, vn_ref, gate_ref, lq1, lk1, lq2, lk2, sg_ref, *rest,
                   pages_per_step, lam0):
    del pt_ref
    np_ = pages_per_step
    k_refs = rest[:np_]
    v_refs = rest[np_:2 * np_]
    o_ref, m_s, l_s, acc_s = rest[2 * np_:]
    step = pl.program_id(1)
    n_steps = pl.num_programs(1)
    qb = qb_ref[...]

    @pl.when(step == 0)
    def _():
        m_s[...] = jnp.sum(qn_ref[...] * kn_ref[...], axis=-1, keepdims=True)
        l_s[...] = jnp.ones(l_s.shape, F32)
        acc_s[...] = vn_ref[...]

    s = jnp.concatenate([jnp.sum(qb * k_refs[j][...], axis=1) for j in range(np_)],
                        axis=-1)
    m_old = m_s[...]
    m_new = jnp.maximum(m_old, jnp.max(s, axis=-1, keepdims=True))
    alpha = jnp.exp(m_old - m_new)
    p = jnp.exp(s - m_new)
    l_s[...] = alpha * l_s[...] + jnp.sum(p, axis=-1, keepdims=True)
    row = lax.broadcasted_iota(jnp.int32, p.shape, 0)
    pv = None
    for hv in range(N_HEADS):
        ph = jnp.where(row // 2 == hv, p, 0.0).astype(BF16)
        vh = jnp.concatenate([r[pl.ds(hv, PAGE_SIZE, stride=N_HEADS), :] for r in v_refs],
                             axis=0).astype(BF16)
        d = _dot(ph, vh)
        pv = d if pv is None else pv + d
    acc_s[...] = alpha * acc_s[...] + pv
    m_s[...] = m_new

    @pl.when(step == n_steps - 1)
    def _():
        lam = _lam(lq1, lk1, lq2, lk2, lam0)
        acc_s[...] = acc_s[...] / l_s[...]
        even = acc_s[pl.ds(0, N_HEADS, stride=2), :]
        odd = acc_s[pl.ds(1, N_HEADS, stride=2), :]
        o = _rms(even - lam * odd, sg_ref[...], SUBLN_EPS) * (1.0 - lam0)
        o_ref[...] = o * gate_ref[...]


def _decode_attn(q32, k_new, v_new, gate32, cache_k, cache_v, page_table, lams, subln_g, lam0,
                 pages_per_step):
    db = q32.shape[0]
    n_pages = page_table.shape[1]
    kt = jnp.transpose(cache_k, (0, 2, 3, 1))
    lane_bcast = lambda a: jnp.broadcast_to(
        a.reshape(db, N_QK_HEADS, HEAD_DIM, 1), (db, N_QK_HEADS, HEAD_DIM, LANES))
    vn = jnp.repeat(v_new.reshape(db, N_HEADS, V_HEAD_DIM), 2, axis=1)
    tok4 = pl.BlockSpec((None, N_QK_HEADS, HEAD_DIM, LANES), lambda b, s, pt: (b, 0, 0, 0))
    small = lambda b, s, pt: (0, 0)

    vr = cache_v.reshape(cache_v.shape[0], PAGE_SIZE * N_HEADS, V_HEAD_DIM)

    def kpage(j):
        return lambda b, s, pt: (pt[b, s * pages_per_step + j], 0, 0, 0)

    def vpage(j):
        return lambda b, s, pt: (pt[b, s * pages_per_step + j], 0, 0)

    k_specs = [pl.BlockSpec((None, N_QK_HEADS, HEAD_DIM, PAGE_SIZE), kpage(j))
               for j in range(pages_per_step)]
    v_specs = [pl.BlockSpec((None, PAGE_SIZE * N_HEADS, V_HEAD_DIM), vpage(j))
               for j in range(pages_per_step)]
    grid_spec = pltpu.PrefetchScalarGridSpec(
        num_scalar_prefetch=1,
        grid=(db, n_pages // pages_per_step),
        in_specs=[tok4,
                  pl.BlockSpec((None, N_QK_HEADS, HEAD_DIM), lambda b, s, pt: (b, 0, 0)),
                  pl.BlockSpec((None, N_QK_HEADS, HEAD_DIM), lambda b, s, pt: (b, 0, 0)),
                  pl.BlockSpec((None, N_QK_HEADS, V_HEAD_DIM), lambda b, s, pt: (b, 0, 0)),
                  pl.BlockSpec((None, N_HEADS, V_HEAD_DIM), lambda b, s, pt: (b, 0, 0)),
                  *[pl.BlockSpec((1, HEAD_DIM), small)] * 4,
                  pl.BlockSpec((1, V_HEAD_DIM), small),
                  *k_specs, *v_specs],
        out_specs=pl.BlockSpec((None, N_HEADS, V_HEAD_DIM), lambda b, s, pt: (b, 0, 0)),
        scratch_shapes=[
            pltpu.VMEM((N_QK_HEADS, 1), F32),
            pltpu.VMEM((N_QK_HEADS, 1), F32),
            pltpu.VMEM((N_QK_HEADS, V_HEAD_DIM), F32),
        ],
    )
    out = pl.pallas_call(
        functools.partial(_decode_kernel, pages_per_step=pages_per_step, lam0=lam0),
        grid_spec=grid_spec,
        out_shape=jax.ShapeDtypeStruct((db, N_HEADS, V_HEAD_DIM), F32),
        compiler_params=pltpu.CompilerParams(
            dimension_semantics=("arbitrary", "arbitrary"),
            vmem_limit_bytes=VMEM_LIMIT_BYTES),
        name="paged_diff_attn",
    )(page_table, lane_bcast(q32), q32.reshape(db, N_QK_HEADS, HEAD_DIM),
      k_new.reshape(db, N_QK_HEADS, HEAD_DIM), vn,
      gate32.reshape(db, N_HEADS, V_HEAD_DIM), *lams, subln_g,
      *([kt] * pages_per_step), *([vr] * pages_per_step))
    return out.reshape(db, D_MODEL)


def _out_proj_kernel(a_ref, w_ref, pg_ref, x_ref, o_ref):
    y = _dot(a_ref[...].astype(BF16), w_ref[...])
    o_ref[...] = x_ref[...] + _rms(y, pg_ref[...], NORM_EPS)


def _out_proj(a, w16, post_g, x, tm):
    m = x.shape[0]
    row = lambda i: (i, 0)
    const = lambda i: (0, 0)
    return pl.pallas_call(
        _out_proj_kernel,
        grid=(m // tm,),
        in_specs=[
            pl.BlockSpec((tm, D_MODEL), row),
            pl.BlockSpec((D_MODEL, D_MODEL), const),
            pl.BlockSpec((1, D_MODEL), const),
            pl.BlockSpec((tm, D_MODEL), row),
        ],
        out_specs=pl.BlockSpec((tm, D_MODEL), row),
        out_shape=jax.ShapeDtypeStruct((m, D_MODEL), F32),
        compiler_params=pltpu.CompilerParams(
            dimension_semantics=("arbitrary",), vmem_limit_bytes=VMEM_LIMIT_BYTES),
        name="out_proj",
    )(a, w16, post_g, x)


def _gmlp_kernel(*refs, tm, single_row_chunks, prev_out_proj):
    if prev_out_proj:
        a_ref, wo_ref, pgo_ref = refs[:3]
        refs = refs[3:]
    x_ref, g_ref, w_in_ref, lng_ref, lnb_ref, ws_ref, bs_ref, w_out_ref, pg_ref, o_ref, *gv_ref = refs
    d = D_MODEL
    x = x_ref[...]
    if prev_out_proj:
        x = x + _rms(_dot(a_ref[...], wo_ref[...]), pgo_ref[...], NORM_EPS)
    hb = _rms(x, g_ref[...], NORM_EPS).astype(BF16)
    u = _gelu(_dot(hb, w_in_ref[:, 0:d]))
    v = _gelu(_dot(hb, w_in_ref[:, d:2 * d]))
    mu = jnp.mean(v, axis=-1, keepdims=True)
    vc = v - mu
    v = vc * lax.rsqrt(jnp.mean(vc * vc, axis=-1, keepdims=True) + LN_EPS) * lng_ref[...] + lnb_ref[...]
    gate = _silu(_dot(hb, w_in_ref[:, 2 * d:3 * d]))
    if single_row_chunks:
        s = v * ws_ref[...] + bs_ref[...]
        gv_ref[0][...] = v
    else:
        r = lax.broadcasted_iota(jnp.int32, (CHUNK, CHUNK), 0)
        c = lax.broadcasted_iota(jnp.int32, (CHUNK, CHUNK), 1)
        vb = v.astype(BF16)
        rows = []
        for ci in range(tm // CHUNK):
            cols = []
            for gi in range(N_GROUPS):
                w = jnp.where(c <= r, ws_ref[gi], 0.0).astype(BF16)
                cols.append(_dot(w, vb[ci * CHUNK:(ci + 1) * CHUNK, gi * GROUP_DIM:(gi + 1) * GROUP_DIM]))
            rows.append(jnp.concatenate(cols, axis=-1) + bs_ref[...])
        s = jnp.concatenate(rows, axis=0)
    z = (u * s * gate).astype(BF16)
    o_ref[...] = x + _rms(_dot(z, w_out_ref[...]), pg_ref[...], NORM_EPS)


def _gmlp(x, pre_g, w_in16, ln_g, ln_b, ws, bs, w_out16, post_g, tm, single_row_chunks,
          prev_out_proj=None):
    m = x.shape[0]
    row = lambda i: (i, 0)
    const = lambda i: (0, 0)
    vec = pl.BlockSpec((1, D_MODEL), const)
    sq = pl.BlockSpec((D_MODEL, D_MODEL), const)
    prev_specs = [pl.BlockSpec((tm, D_MODEL), row), sq, vec] if prev_out_proj else []
    prev_args = list(prev_out_proj) if prev_out_proj else []
    if single_row_chunks:
        ws_spec = vec
        bs_spec = vec
        out_shape = [jax.ShapeDtypeStruct((m, D_MODEL), F32)] * 2
        out_specs = [pl.BlockSpec((tm, D_MODEL), row)] * 2
    else:
        ws_spec = pl.BlockSpec((N_GROUPS, CHUNK, CHUNK), lambda i: (0, 0, 0))
        bs_spec = pl.BlockSpec((CHUNK, D_MODEL), const)
        out_shape = jax.ShapeDtypeStruct((m, D_MODEL), F32)
        out_specs = pl.BlockSpec((tm, D_MODEL), row)
    return pl.pallas_call(
        functools.partial(_gmlp_kernel, tm=tm, single_row_chunks=single_row_chunks,
                          prev_out_proj=bool(prev_out_proj)),
        grid=(m // tm,),
        in_specs=[
            *prev_specs,
            pl.BlockSpec((tm, D_MODEL), row),
            vec,
            pl.BlockSpec((D_MODEL, 3 * D_MODEL), const),
            vec, vec, ws_spec, bs_spec,
            sq,
            vec,
        ],
        out_specs=out_specs,
        out_shape=out_shape,
        compiler_params=pltpu.CompilerParams(
            dimension_semantics=("arbitrary",), vmem_limit_bytes=VMEM_LIMIT_BYTES),
        name="gmlp_rows" if single_row_chunks else "gmlp_chunks",
    )(*prev_args, x, pre_g, w_in16, ln_g, ln_b, ws, bs, w_out16, post_g)


PROMPT_TM = 512
FLASH_TK = 512
FLASH_TQ = 2 * FLASH_TK
PAGES_PER_STEP = 16


def kernel(x_prompt, x_sample, cache_k_l0, cache_v_l0, cache_k_l2, cache_v_l2, page_table, pre_norm_g, post_norm_g, attn_w_in, attn_lambda_q1, attn_lambda_k1, attn_lambda_q2, attn_lambda_k2, attn_subln_g, attn_w_out, gmlp_w_in, gmlp_ln_g, gmlp_ln_b, gmlp_w_s, gmlp_b_s, gmlp_w_out):
    batch, seq, d = x_prompt.shape
    db, dec_seq, _ = x_sample.shape
    assert dec_seq == 1 and d == D_MODEL
    past = page_table.shape[1] * PAGE_SIZE
    depth = pre_norm_g.shape[0]
    caches = ((cache_k_l0, cache_v_l0), (cache_k_l2, cache_v_l2))

    cos_p, sin_p = _rope_tables(jnp.arange(seq))
    cos_s, sin_s = _rope_tables(jnp.full((db,), past))

    xp = x_prompt.reshape(batch * seq, d)
    xs = x_sample.reshape(db, d)
    vec = lambda a: a.reshape(1, -1)
    kp_out, vp_out, ks_out, vs_out, gv_out = [], [], [], [], []
    pending = None
    for i in range(depth):
        li = i // 2
        pre_g = vec(pre_norm_g[i])
        post_g = vec(post_norm_g[i])
        if i % 2 == 0:
            lam0 = _lambda_init(i)
            lams = [vec(a[li]) for a in (attn_lambda_q1, attn_lambda_k1, attn_lambda_q2, attn_lambda_k2)]
            w_in16 = attn_w_in[li].astype(BF16)
            w_out16 = attn_w_out[li].astype(BF16)
            subln = vec(attn_subln_g[li])
            q16, kt32, k16, v32, vt16, gate16 = _attn_in(xp, pre_g, w_in16, cos_p, sin_p, PROMPT_TM,
                                                         kv_chunk=FLASH_TK)
            og = _flash(q16, k16, vt16, gate16, lams, subln, lam0, batch, seq, FLASH_TQ, FLASH_TK)
            if i + 1 < depth:
                pending = (og, w_out16, post_g)
            else:
                xp = _out_proj(og, w_out16, post_g, xp, PROMPT_TM)
            kp_out.append(jnp.transpose(kt32.reshape(batch, N_QK_HEADS, HEAD_DIM, seq), (0, 3, 1, 2)))
            vp_out.append(v32.reshape(batch, seq, N_HEADS, V_HEAD_DIM))

            qs32, ks32, vs32, gs32 = _attn_in(xs, pre_g, w_in16, cos_s, sin_s, db)
            ck, cv = caches[li]
            ogs = _decode_attn(qs32, ks32, vs32, gs32, ck, cv, page_table, lams, subln, lam0,
                               PAGES_PER_STEP)
            xs = _out_proj(ogs, w_out16, post_g, xs, db)
            ks_out.append(ks32.reshape(db, 1, N_QK_HEADS, HEAD_DIM))
            vs_out.append(vs32.reshape(db, 1, N_HEADS, V_HEAD_DIM))
        else:
            w_in16 = gmlp_w_in[li].astype(BF16)
            w_out16 = gmlp_w_out[li].astype(BF16)
            ln_g = vec(gmlp_ln_g[li])
            ln_b = vec(gmlp_ln_b[li])
            ws = gmlp_w_s[li]
            bs = gmlp_b_s[li]
            bs_full = jnp.repeat(bs.T, GROUP_DIM, axis=1)
            xp = _gmlp(xp, pre_g, w_in16, ln_g, ln_b, ws, bs_full, w_out16, post_g, PROMPT_TM, False,
                       prev_out_proj=pending)
            pending = None
            ws0 = vec(jnp.repeat(ws[:, 0, 0], GROUP_DIM))
            bs0 = vec(jnp.repeat(bs[:, 0], GROUP_DIM))
            xs, gv = _gmlp(xs, pre_g, w_in16, ln_g, ln_b, ws0, bs0, w_out16, post_g, db, True)
            gv_out.append(gv.reshape(db, 1, d))

    return (xp.reshape(batch, seq, d), xs.reshape(db, 1, d),
            kp_out[0], vp_out[0], ks_out[0], vs_out[0], gv_out[0],
            kp_out[1], vp_out[1], ks_out[1], vs_out[1], gv_out[1])
```
